```python
import jax, jax.numpy as jnp
from jax import lax
import numpy as np

D_MODEL = 2048
BATCH = 4
SEQ = 8192
DEPTH = 1

CHUNK = 64
D_FF = 5632
FFN_RES = 0.5
RMS_EPS = 1e-6

RWKV_WIDTH = D_MODEL // 2
RWKV_HEAD = 64
RWKV_HEADS = RWKV_WIDTH // RWKV_HEAD
DECAY_LORA = 64
AAA_LORA = 64
GATE_LORA = 160
GN_EPS = 64e-5

HGRN_WIDTH = D_MODEL - RWKV_WIDTH
HGRN_EXPAND = 128
HGRN_HEADS = HGRN_WIDTH // HGRN_EXPAND
HGRN_HEAD = HGRN_WIDTH // HGRN_HEADS

RWKV_COLS = 3 * RWKV_WIDTH + DECAY_LORA + AAA_LORA + GATE_LORA
HGRN_COLS = 4 * HGRN_WIDTH
IN_COLS = RWKV_COLS + HGRN_COLS

kernel_name = 'hybrid_rwkv7_hgrn2_macaron_layer'


def rms_norm(x, gain, eps=RMS_EPS):
    x32 = x.astype(jnp.float32)
    inv = lax.rsqrt(jnp.mean(x32 * x32, axis=-1, keepdims=True) + eps)
    return (x32 * inv).astype(x.dtype) * gain


def swiglu(h, w_gate, w_up, w_down):
    return (jax.nn.silu(h @ w_gate) * (h @ w_up)) @ w_down


def rwkv7_mix(p, mu, w0, w2, a0, a2, g2, k_k, k_a, r_k, ln_w, ln_b):
    B, T, _ = p.shape
    C, H, N = RWKV_WIDTH, RWKV_HEADS, RWKV_HEAD
    f32 = jnp.float32
    p_prev = jnp.pad(p, ((0, 0), (1, 0), (0, 0)))[:, :T]
    p = p + (p_prev - p) * mu
    r, k, v, xw, xa, xg = jnp.split(
        p, [C, 2 * C, 3 * C, 3 * C + DECAY_LORA, 3 * C + DECAY_LORA + AAA_LORA], axis=-1)
    w = -jax.nn.softplus(-(w0 + jnp.tanh(xw.astype(f32)) @ w2.astype(f32))) - 0.5
    decay = jnp.exp(-jnp.exp(w))
    a = jax.nn.sigmoid((a0 + xa @ a2).astype(f32))
    g = jax.nn.sigmoid(xg) @ g2
    heads = lambda t: t.astype(f32).reshape(B, T, H, N)
    r, k, v, a, decay = heads(r), heads(k), heads(v), heads(a), heads(decay)
    kk = k * k_k.astype(f32).reshape(H, N)
    kk = kk / jnp.maximum(jnp.sqrt(jnp.sum(kk * kk, axis=-1, keepdims=True)), 1e-12)
    k = k * (1.0 + (a - 1.0) * k_a.astype(f32).reshape(H, N))

    def step(S, inp):
        r_t, w_t, k_t, v_t, kk_t, a_t = inp
        sa = jnp.einsum('bhvk,bhk->bhv', S, -kk_t)
        S = (S * w_t[:, :, None, :] + sa[..., None] * (kk_t * a_t)[:, :, None, :]
             + v_t[..., None] * k_t[:, :, None, :])
        return S, jnp.einsum('bhvk,bhk->bhv', S, r_t)

    seq_first = lambda t: jnp.moveaxis(t, 1, 0)
    S0 = jnp.zeros((B, H, N, N), f32)
    _, y = lax.scan(step, S0, (seq_first(r), seq_first(decay), seq_first(k),
                               seq_first(v), seq_first(kk), seq_first(a)))
    y = jnp.moveaxis(y, 0, 1)
    mean = jnp.mean(y, axis=-1, keepdims=True)
    var = jnp.var(y, axis=-1, keepdims=True)
    y = ((y - mean) * lax.rsqrt(var + GN_EPS)).reshape(B, T, C) * ln_w + ln_b
    bonus = jnp.sum(r * k * r_k.astype(f32), axis=-1, keepdims=True) * v
    y = (y + bonus.reshape(B, T, C)) * g
    return y.astype(p.dtype)


def hgrn2_mix(p, lb, norm_w):
    B, T, _ = p.shape
    H, Dh = HGRN_HEADS, HGRN_HEAD
    f32 = jnp.float32
    q, f, i, gate = jnp.split(p, 4, axis=-1)
    q = jax.nn.silu(q.astype(f32))
    forget = lb + (1.0 - lb) * jax.nn.sigmoid(f.astype(f32))
    log_f = jnp.log(forget)
    k = 1.0 - forget
    n_chunks = T // CHUNK
    chunks = lambda t: t.astype(f32).reshape(B, n_chunks, CHUNK, H, Dh).transpose(1, 0, 3, 2, 4)
    causal = jnp.tril(jnp.ones((CHUNK, CHUNK), dtype=bool))

    def step(S, inp):
        q_c, k_c, v_c, lf_c = inp
        G = jnp.cumsum(lf_c, axis=2)
        o_inter = jnp.einsum('bhik,bhkv->bhiv', q_c * jnp.exp(G), S)
        diff = G[:, :, :, None, :] - G[:, :, None, :, :]
        dec = jnp.exp(jnp.where(causal[:, :, None], diff, -jnp.inf))
        A = jnp.einsum('bhik,bhijk->bhij', q_c, dec * k_c[:, :, None, :, :])
        o = o_inter + jnp.einsum('bhij,bhjv->bhiv', A, v_c)
        G_last = G[:, :, -1:, :]
        S = (S * jnp.exp(G_last)[:, :, 0, :, None]
             + jnp.einsum('bhjk,bhjv->bhkv', k_c * jnp.exp(G_last - G), v_c))
        return S, o

    S0 = jnp.zeros((B, H, Dh, Dh), f32)
    _, o = lax.scan(step, S0, (chunks(q), chunks(k), chunks(i), chunks(log_f)))
    o = o.transpose(1, 0, 3, 2, 4).reshape(B, T, H, Dh)
    o = o * lax.rsqrt(jnp.mean(o * o, axis=-1, keepdims=True) + RMS_EPS)
    o = o.reshape(B, T, HGRN_WIDTH) * norm_w * jax.nn.silu(gate.astype(f32))
    return o.astype(p.dtype)


def setup_inputs(seed: int = 0) -> dict:
    key = jax.random.key(seed)
    ks = jax.random.split(key, 32)
    f32 = jnp.float32
    L, D, F = DEPTH, D_MODEL, D_FF
    C, Ch = RWKV_WIDTH, HGRN_WIDTH
    nrm = lambda k, shape, scale: scale * jax.random.normal(k, shape, f32)
    gain = lambda k, shape: 1.0 + 0.02 * jax.random.normal(k, shape, f32)
    return {
        'x': nrm(ks[0], (BATCH, SEQ, D), 1.0),
        'ffn1_norm': gain(ks[1], (L, D)),
        'ffn1_w_gate': nrm(ks[2], (L, D, F), D ** -0.5),
        'ffn1_w_up': nrm(ks[3], (L, D, F), D ** -0.5),
        'ffn1_w_down': nrm(ks[4], (L, F, D), F ** -0.5),
        'mix_norm': gain(ks[5], (L, D)),
        'w_in': nrm(ks[6], (L, D, IN_COLS), D ** -0.5),
        'rwkv_mu': jax.random.uniform(ks[7], (L, RWKV_COLS), f32),
        'rwkv_w0': jax.random.uniform(ks[8], (L, C), f32, -6.0, -1.0),
        'rwkv_w2': nrm(ks[9], (L, DECAY_LORA, C), 0.1),
        'rwkv_a0': nrm(ks[10], (L, C), 0.1),
        'rwkv_a2': nrm(ks[11], (L, AAA_LORA, C), 0.1),
        'rwkv_g2': nrm(ks[12], (L, GATE_LORA, C), GATE_LORA ** -0.5),
        'rwkv_k_k': 0.85 + nrm(ks[13], (L, C), 0.05),
        'rwkv_k_a': 1.0 + nrm(ks[14], (L, C), 0.05),
        'rwkv_r_k': nrm(ks[15], (L, RWKV_HEADS, RWKV_HEAD), 0.1),
        'rwkv_ln_w': gain(ks[16], (L, C)),
        'rwkv_ln_b': nrm(ks[17], (L, C), 0.02),
        'hgrn_lb_logits': nrm(ks[18], (L + 1, Ch), 0.5),
        'hgrn_norm': gain(ks[19], (L, Ch)),
        'w_out': nrm(ks[20], (L, D, D), D ** -0.5),
        'ffn2_norm': gain(ks[21], (L, D)),
        'ffn2_w_gate': nrm(ks[22], (L, D, F), D ** -0.5),
        'ffn2_w_up': nrm(ks[23], (L, D, F), D ** -0.5),
        'ffn2_w_down': nrm(ks[24], (L, F, D), F ** -0.5),
        'final_norm': gain(ks[25], (D,)),
    }


def reference(x, ffn1_norm, ffn1_w_gate, ffn1_w_up, ffn1_w_down, mix_norm, w_in,
              rwkv_mu, rwkv_w0, rwkv_w2, rwkv_a0, rwkv_a2, rwkv_g2, rwkv_k_k, rwkv_k_a,
              rwkv_r_k, rwkv_ln_w, rwkv_ln_b, hgrn_lb_logits, hgrn_norm, w_out,
              ffn2_norm, ffn2_w_gate, ffn2_w_up, ffn2_w_down, final_norm):
    lb_all = jnp.cumsum(jax.nn.softmax(hgrn_lb_logits.astype(jnp.float32), axis=0), axis=0)
    for l in range(DEPTH):
        h = rms_norm(x, ffn1_norm[l])
        x = x + FFN_RES * swiglu(h, ffn1_w_gate[l], ffn1_w_up[l], ffn1_w_down[l])
        h = rms_norm(x, mix_norm[l])
        proj = h @ w_in[l]
        y_rwkv = rwkv7_mix(proj[..., :RWKV_COLS], rwkv_mu[l], rwkv_w0[l], rwkv_w2[l],
                           rwkv_a0[l], rwkv_a2[l], rwkv_g2[l], rwkv_k_k[l], rwkv_k_a[l],
                           rwkv_r_k[l], rwkv_ln_w[l], rwkv_ln_b[l])
        y_hgrn = hgrn2_mix(proj[..., RWKV_COLS:], lb_all[l], hgrn_norm[l])
        x = x + jnp.concatenate([y_rwkv, y_hgrn], axis=-1) @ w_out[l]
        h = rms_norm(x, ffn2_norm[l])
        x = x + FFN_RES * swiglu(h, ffn2_w_gate[l], ffn2_w_up[l], ffn2_w_down[l])
    return rms_norm(x, final_norm)
```

```python
import functools

import numpy as np
import jax
import jax.numpy as jnp
from jax import lax
from jax.experimental import pallas as pl
from jax.experimental.pallas import tpu as pltpu

F32 = jnp.float32
BF16 = jnp.bfloat16

RMS_EPS = 1e-6
GN_EPS = 64e-5
FFN_RES = 0.5
KK_EPS = 1e-12

LANES = 128
RW_HEAD = 64
HG_HEAD = 128
CHUNK = 64
STACK = 2 * CHUNK
VMEM_LIMIT = 56 * 1024 * 1024

HG_LEVELS = (32, 16, 8, 4, 2, 1)


def _dot(a, b):
    return jnp.dot(a, b, preferred_element_type=F32)


def _dot_nt(a, b):
    return lax.dot_general(a, b, (((1,), (1,)), ((), ())), preferred_element_type=F32)


def _dot_tn(a, b):
    return lax.dot_general(a, b, (((0,), (0,)), ((), ())), preferred_element_type=F32)


def _bdot(a, b):
    return _dot(a.astype(BF16), b.astype(BF16))


def _split2(x):
    hi = x.astype(BF16)
    lo = (x - hi.astype(F32)).astype(BF16)
    return hi, lo


def _dot_sel(sel, x):
    hi, lo = _split2(x)
    return _dot(sel, hi) + _dot(sel, lo)


def _dot_sel_r(x, sel):
    hi, lo = _split2(x)
    return _dot(hi, sel) + _dot(lo, sel)


def _sigmoid(x):
    return 1.0 / (1.0 + jnp.exp(-x))


def _silu(x):
    return x * _sigmoid(x)


def _cparams(n_axes):
    return pltpu.CompilerParams(dimension_semantics=("arbitrary",) * n_axes,
                                vmem_limit_bytes=VMEM_LIMIT)


def _ffn_kernel(x_ref, g_ref, wg_ref, wu_ref, wd_ref, fn_ref, o_ref, h_ref, acc_ref, *, final_norm):
    j = pl.program_id(1)

    @pl.when(j == 0)
    def _():
        x = x_ref[...]
        inv = lax.rsqrt(jnp.mean(x * x, axis=-1, keepdims=True) + RMS_EPS)
        h_ref[...] = ((x * inv) * g_ref[...]).astype(BF16)
        acc_ref[...] = jnp.zeros_like(acc_ref)

    h = h_ref[...]
    gate = _dot(h, wg_ref[...])
    up = _dot(h, wu_ref[...])
    act = (_silu(gate) * up).astype(BF16)
    acc_ref[...] += _dot(act, wd_ref[...])

    @pl.when(j == pl.num_programs(1) - 1)
    def _():
        y = x_ref[...] + FFN_RES * acc_ref[...]
        if final_norm:
            inv = lax.rsqrt(jnp.mean(y * y, axis=-1, keepdims=True) + RMS_EPS)
            y = (y * inv) * fn_ref[...]
        o_ref[...] = y


def _ffn(x, gain, wg, wu, wd, fnorm, *, final_norm, tm=512, tf=512):
    n, d = x.shape
    f = wg.shape[1]
    assert n % tm == 0 and f % tf == 0
    return pl.pallas_call(
        functools.partial(_ffn_kernel, final_norm=final_norm),
        grid=(n // tm, f // tf),
        in_specs=[
            pl.BlockSpec((tm, d), lambda i, j: (i, 0)),
            pl.BlockSpec((1, d), lambda i, j: (0, 0)),
            pl.BlockSpec((d, tf), lambda i, j: (0, j)),
            pl.BlockSpec((d, tf), lambda i, j: (0, j)),
            pl.BlockSpec((tf, d), lambda i, j: (j, 0)),
            pl.BlockSpec((1, d), lambda i, j: (0, 0)),
        ],
        out_specs=pl.BlockSpec((tm, d), lambda i, j: (i, 0)),
        out_shape=jax.ShapeDtypeStruct((n, d), F32),
        scratch_shapes=[pltpu.VMEM((tm, d), BF16), pltpu.VMEM((tm, d), F32)],
        compiler_params=_cparams(2),
        name="ffn_final" if final_norm else "ffn",
    )(x, gain, wg, wu, wd, fnorm)


def _inproj_kernel(x_ref, g_ref, w_ref, o_ref, h_ref):
    @pl.when(pl.program_id(1) == 0)
    def _():
        x = x_ref[...]
        inv = lax.rsqrt(jnp.mean(x * x, axis=-1, keepdims=True) + RMS_EPS)
        h_ref[...] = ((x * inv) * g_ref[...]).astype(BF16)

    o_ref[...] = _dot(h_ref[...], w_ref[...])


def _inproj(x, gain, w, *, tm=512, tn=1536):
    n, d = x.shape
    c = w.shape[1]
    assert n % tm == 0 and c % tn == 0
    return pl.pallas_call(
        _inproj_kernel,
        grid=(n // tm, c // tn),
        in_specs=[
            pl.BlockSpec((tm, d), lambda i, j: (i, 0)),
            pl.BlockSpec((1, d), lambda i, j: (0, 0)),
            pl.BlockSpec((d, tn), lambda i, j: (0, j)),
        ],
        out_specs=pl.BlockSpec((tm, tn), lambda i, j: (i, j)),
        out_shape=jax.ShapeDtypeStruct((n, c), F32),
        scratch_shapes=[pltpu.VMEM((tm, d), BF16)],
        compiler_params=_cparams(2),
        name="inproj",
    )(x, gain, w)


def _outproj_kernel(x_ref, yr_ref, yh_ref, wr_ref, wh_ref, o_ref):
    o_ref[...] = x_ref[...] + _dot(yr_ref[...], wr_ref[...]) + _dot(yh_ref[...], wh_ref[...])


def _outproj(x, yr, yh, wr, wh, *, tm=512):
    n, d = x.shape
    cr, ch = yr.shape[1], yh.shape[1]
    assert n % tm == 0
    return pl.pallas_call(
        _outproj_kernel,
        grid=(n // tm,),
        in_specs=[
            pl.BlockSpec((tm, d), lambda i: (i, 0)),
            pl.BlockSpec((tm, cr), lambda i: (i, 0)),
            pl.BlockSpec((tm, ch), lambda i: (i, 0)),
            pl.BlockSpec((cr, d), lambda i: (0, 0)),
            pl.BlockSpec((ch, d), lambda i: (0, 0)),
        ],
        out_specs=pl.BlockSpec((tm, d), lambda i: (i, 0)),
        out_shape=jax.ShapeDtypeStruct((n, d), F32),
        compiler_params=_cparams(1),
        name="outproj",
    )(x, yr, yh, wr, wh)


def _rwkv_masks():
    i = np.arange(STACK)[:, None]
    j = np.arange(STACK)[None, :]
    same = (i // CHUNK) == (j // CHUNK)
    sl = same & (i > j)
    il = same & (i >= j)
    blk = lambda s: (i // s) == (j // s)
    d8 = sl & blk(8)
    c16 = sl & blk(16) & ~blk(8)
    c32 = sl & blk(32) & ~blk(16)
    c64 = sl & ~blk(32)
    return np.stack([sl, il, d8, c16, c32, c64]).astype(np.float32)


def _head_sum_matrix():
    i = np.arange(LANES)[:, None]
    j = np.arange(LANES)[None, :]
    return ((i // RW_HEAD) == (j // RW_HEAD)).astype(np.float32)


def _tri64():
    i = np.arange(CHUNK)[:, None]
    j = np.arange(CHUNK)[None, :]
    return (i >= j).astype(np.float32)


def _hgrn_consts():
    i = np.arange(STACK)[:, None]
    j = np.arange(STACK)[None, :]
    same = (i // CHUNK) == (j // CHUNK)
    sels = [same & (i >= j)]
    masks = []
    for h in HG_LEVELS:
        m = (i // (2 * h)) * (2 * h) + h - 1
        sels.append(same & (j <= m))
        masks.append(((i // (2 * h)) == (j // (2 * h))) & ((i % (2 * h)) >= h) & ((j % (2 * h)) < h))
    masks.append(i == j)
    return (np.concatenate(sels, axis=0).astype(np.float32), np.stack(masks).astype(np.float32))


def _shift(x, prev8, mu, first):
    rows = lax.broadcasted_iota(jnp.int32, x.shape, 0)
    prev_row = jnp.where(first, 0.0, prev8[7:8, :])
    xp = jnp.where(rows == 0, prev_row, pltpu.roll(x, shift=1, axis=0))
    return x + (xp - x) * mu


def _stack2(x):
    lane = lax.broadcasted_iota(jnp.int32, x.shape, 1)
    lo = lane < RW_HEAD
    return jnp.concatenate([jnp.where(lo, x, 0.0), jnp.where(lo, 0.0, x)], axis=0)


def _unit_lower_inverse(m, mk_ref):
    eye = (lax.broadcasted_iota(jnp.int32, m.shape, 0) == lax.broadcasted_iota(jnp.int32, m.shape, 1)).astype(F32)
    m8 = m * mk_ref[2]
    m8_2 = _bdot(m8, m8)
    m8_4 = _bdot(m8_2, m8_2)
    t = eye + m8
    t = t + _bdot(t, m8_2)
    t = t + _bdot(t, m8_4)
    for lvl in (3, 4, 5):
        c = m * mk_ref[lvl]
        t = t + _bdot(_bdot(t, c), t)
    return t


def _rwkv_kernel(r_ref, k_ref, v_ref, lo_ref, rp_ref, kp_ref, vp_ref, lop_ref,
                 mur_ref, muk_ref, muv_ref, mul_ref,
                 w0_ref, a0_ref, kk_ref, ka_ref, rk_ref, lnw_ref, lnb_ref,
                 w2_ref, a2_ref, g2_ref, tri_ref, e2_ref, mk_ref,
                 y_ref,
                 s_ref, r_s, k_s, v_s, al_s, be_s, lw_s, y_s):
    t = pl.program_id(1)
    p = pl.program_id(2)
    first = t == 0
    tb = r_ref.shape[0]
    nch = tb // CHUNK

    @pl.when(first)
    def _():
        s_ref[p] = jnp.zeros((LANES, LANES), F32)

    e2 = e2_ref[...]
    r = _shift(r_ref[...], rp_ref[...], mur_ref[...], first)
    k = _shift(k_ref[...], kp_ref[...], muk_ref[...], first)
    v = _shift(v_ref[...], vp_ref[...], muv_ref[...], first)
    lora = _shift(lo_ref[...], lop_ref[...], mul_ref[...], first)
    xw = lora[:, 0:LANES]
    xa = lora[:, LANES:2 * LANES]
    xg = lora[:, 2 * LANES:4 * LANES]

    z = w0_ref[...] + _dot(jnp.tanh(xw).astype(BF16), w2_ref[...])
    w = -(jnp.maximum(-z, 0.0) + jnp.log(1.0 + jnp.exp(-jnp.abs(z)))) - 0.5
    lw = -jnp.exp(w)
    a = _sigmoid(a0_ref[...] + _dot(xa.astype(BF16), a2_ref[...]))
    g = _dot(_sigmoid(xg).astype(BF16), g2_ref[...])

    kk = k * kk_ref[...]
    nrm = jnp.sqrt(_dot_sel_r(kk * kk, e2))
    kk = kk / jnp.maximum(nrm, KK_EPS)
    kmod = k * (1.0 + (a - 1.0) * ka_ref[...])
    bonus = _dot_sel_r(r * kmod * rk_ref[...], e2) * v

    r_s[...] = r
    k_s[...] = kmod
    v_s[...] = v
    al_s[...] = -kk
    be_s[...] = kk * a
    lw_s[...] = lw

    tri = tri_ref[...]

    def chunk(c, carry):
        rows = pl.ds(pl.multiple_of(c * CHUNK, CHUNK), CHUNK)
        rc, kc, vc = r_s[rows, :], k_s[rows, :], v_s[rows, :]
        alc, bec, lwc = al_s[rows, :], be_s[rows, :], lw_s[rows, :]
        gc = _dot_sel(tri, lwc)
        glast = gc[CHUNK - 1:CHUNK, :]
        e_in = jnp.exp(gc)
        e_ex = jnp.exp(gc - lwc)
        e_inv = jnp.exp(-gc)
        e_end = jnp.exp(glast - gc)
        at = _stack2(alc * e_ex)
        rt = _stack2(rc * e_in)
        bt = _stack2(bec * e_inv)
        kt = _stack2(kc * e_inv)
        bh = _stack2(bec * e_end)
        kh = _stack2(kc * e_end)
        vs = _stack2(vc)

        g1 = _dot_nt(jnp.concatenate([at, rt], axis=0).astype(BF16),
                     jnp.concatenate([bt, kt], axis=0).astype(BF16))
        m_ab = g1[0:STACK, 0:STACK] * mk_ref[0]
        m_ak = g1[0:STACK, STACK:2 * STACK] * mk_ref[0]
        m_rb = g1[STACK:2 * STACK, 0:STACK] * mk_ref[1]
        m_rk = g1[STACK:2 * STACK, STACK:2 * STACK] * mk_ref[1]

        tinv = _unit_lower_inverse(m_ab, mk_ref)
        x = _bdot(m_ak, vs)
        tx = _bdot(tinv, jnp.concatenate([x, at], axis=1))
        u_loc, wm = tx[:, 0:LANES], tx[:, LANES:2 * LANES]
        y_loc = _bdot(jnp.concatenate([m_rb, m_rk], axis=1), jnp.concatenate([u_loc, vs], axis=0))
        q = rt + _bdot(m_rb, wm)

        s = s_ref[p]
        wqs = _dot_nt(jnp.concatenate([wm, q], axis=0).astype(BF16), s.astype(BF16))
        u = u_loc + wqs[0:STACK]
        y = y_loc + wqs[STACK:2 * STACK]
        s_ref[p] = s * jnp.exp(glast) + _dot_tn(jnp.concatenate([u, vs], axis=0).astype(BF16),
                                                jnp.concatenate([bh, kh], axis=0).astype(BF16))
        y_s[rows, :] = y[0:CHUNK] + y[CHUNK:STACK]
        return carry

    lax.fori_loop(0, nch, chunk, 0)

    y = y_s[...]
    inv_n = 1.0 / RW_HEAD
    mean = _dot_sel_r(y, e2) * inv_n
    d = y - mean
    var = _dot_sel_r(d * d, e2) * inv_n
    yn = d * lax.rsqrt(var + GN_EPS) * lnw_ref[...] + lnb_ref[...]
    y_ref[...] = ((yn + bonus) * g).astype(y_ref.dtype)


def _rwkv(proj, mu, w0, a0, k_k, k_a, r_k, ln_w, ln_b, w2p, a2p, g2p, *, batch, seq, width, lora_col, tb=512):
    n = proj.shape[0]
    nt = seq // tb
    npair = width // LANES
    wb = width // LANES
    lb4 = lora_col // (4 * LANES)
    rb8 = tb // 8

    def main(sec):
        return pl.BlockSpec((tb, LANES), lambda b, t, p: (b * nt + t, sec * wb + p))

    def prev(sec):
        return pl.BlockSpec((8, LANES), lambda b, t, p: (jnp.maximum((b * nt + t) * rb8 - 1, 0), sec * wb + p))

    def vec(sec):
        return pl.BlockSpec((1, LANES), lambda b, t, p: (0, sec * wb + p))

    pvec = pl.BlockSpec((1, LANES), lambda b, t, p: (0, p))
    const2 = lambda shape: pl.BlockSpec(shape, lambda b, t, p: (0, 0))
    tri = jnp.asarray(_tri64(), BF16)
    e2 = jnp.asarray(_head_sum_matrix(), BF16)
    masks = jnp.asarray(_rwkv_masks(), F32)
    blk = lambda: pltpu.VMEM((tb, LANES), F32)
    return pl.pallas_call(
        _rwkv_kernel,
        grid=(batch, nt, npair),
        in_specs=[
            main(0), main(1), main(2),
            pl.BlockSpec((tb, 4 * LANES), lambda b, t, p: (b * nt + t, lb4)),
            prev(0), prev(1), prev(2),
            pl.BlockSpec((8, 4 * LANES), lambda b, t, p: (jnp.maximum((b * nt + t) * rb8 - 1, 0), lb4)),
            vec(0), vec(1), vec(2),
            pl.BlockSpec((1, 4 * LANES), lambda b, t, p: (0, lb4)),
            pvec, pvec, pvec, pvec, pvec, pvec, pvec,
            pl.BlockSpec((LANES, LANES), lambda b, t, p: (0, p)),
            pl.BlockSpec((LANES, LANES), lambda b, t, p: (0, p)),
            pl.BlockSpec((2 * LANES, LANES), lambda b, t, p: (0, p)),
            const2((CHUNK, CHUNK)), const2((LANES, LANES)),
            pl.BlockSpec((6, STACK, STACK), lambda b, t, p: (0, 0, 0)),
        ],
        out_specs=pl.BlockSpec((tb, LANES), lambda b, t, p: (b * nt + t, p)),
        out_shape=jax.ShapeDtypeStruct((n, width), BF16),
        scratch_shapes=[pltpu.VMEM((npair, LANES, LANES), F32)] + [blk() for _ in range(7)],
        compiler_params=_cparams(3),
        name="rwkv",
    )(proj, proj, proj, proj, proj, proj, proj, proj, mu, mu, mu, mu,
      w0, a0, k_k, k_a, r_k, ln_w, ln_b, w2p, a2p, g2p, tri, e2, masks)


def _hgrn_kernel(q_ref, f_ref, i_ref, g_ref, lg_ref, nw_ref, sel_ref, mk_ref,
                 y_ref,
                 s_ref, q_s, k_s, v_s, lf_s, o_s, *, layer):
    t = pl.program_id(1)
    hp = pl.program_id(2)
    tb = q_ref.shape[0]
    nch = tb // CHUNK
    nlev = len(HG_LEVELS)

    @pl.when(t == 0)
    def _():
        s_ref[hp] = jnp.zeros((2, HG_HEAD, HG_HEAD), F32)

    lg = lg_ref[...]
    mx = jnp.max(lg, axis=0, keepdims=True)
    ex = jnp.exp(lg - mx)
    lb = jnp.sum(ex[0:layer + 1, :], axis=0, keepdims=True) / jnp.sum(ex, axis=0, keepdims=True)
    forget = lb + (1.0 - lb) * _sigmoid(f_ref[...])
    q_s[...] = _silu(q_ref[...])
    k_s[...] = 1.0 - forget
    v_s[...] = i_ref[...]
    lf_s[...] = jnp.log(forget)

    def rows2(ref, rows):
        blk = ref[rows, :]
        return jnp.concatenate([blk[:, 0:HG_HEAD], blk[:, HG_HEAD:2 * HG_HEAD]], axis=0)

    def chunk(c, carry):
        rows = pl.ds(pl.multiple_of(c * CHUNK, CHUNK), CHUNK)
        qc, kc, vc, lfc = rows2(q_s, rows), rows2(k_s, rows), rows2(v_s, rows), rows2(lf_s, rows)
        gall = _dot_sel(sel_ref[...], lfc)
        gc = gall[0:STACK]
        a = _dot_nt(qc.astype(BF16), kc.astype(BF16)) * mk_ref[nlev]
        for li in range(nlev):
            gref = gall[(li + 1) * STACK:(li + 2) * STACK]
            qh = qc * jnp.exp(jnp.minimum(gc - gref, 0.0))
            kh = kc * jnp.exp(jnp.minimum(gref - gc, 0.0))
            a = a + _dot_nt(qh.astype(BF16), kh.astype(BF16)) * mk_ref[li]
        o = _bdot(a, vc)
        qg = (qc * jnp.exp(gc)).astype(BF16)
        s_pair = s_ref[hp]
        outs = []
        new = []
        for h in range(2):
            sl = slice(h * CHUNK, (h + 1) * CHUNK)
            s = s_pair[h]
            glast = gc[(h + 1) * CHUNK - 1:(h + 1) * CHUNK, :]
            outs.append(o[sl] + _dot_nt(qg[sl], s.astype(BF16)))
            khat = kc[sl] * jnp.exp(glast - gc[sl])
            new.append(s * jnp.exp(glast) + _dot_tn(vc[sl].astype(BF16), khat.astype(BF16)))
        s_ref[hp] = jnp.stack(new)
        o_s[rows, :] = jnp.concatenate(outs, axis=1)
        return carry

    lax.fori_loop(0, nch, chunk, 0)

    o = o_s[...]
    gate = _silu(g_ref[...])
    nw = nw_ref[...]
    parts = []
    for h in range(2):
        oh = o[:, h * HG_HEAD:(h + 1) * HG_HEAD]
        inv = lax.rsqrt(jnp.mean(oh * oh, axis=-1, keepdims=True) + RMS_EPS)
        parts.append(oh * inv)
    y_ref[...] = (jnp.concatenate(parts, axis=1) * nw * gate).astype(y_ref.dtype)


def _hgrn(proj, lb_logits, norm_w, *, layer, batch, seq, width, col0, tb=512):
    n = proj.shape[0]
    nt = seq // tb
    w2 = 2 * HG_HEAD
    npair = width // w2
    cb = col0 // w2
    sec = width // w2
    sel_np, mk_np = _hgrn_consts()
    sel = jnp.asarray(sel_np, BF16)
    masks = jnp.asarray(mk_np, F32)

    def main(s):
        return pl.BlockSpec((tb, w2), lambda b, t, h: (b * nt + t, cb + s * sec + h))

    blk = lambda: pltpu.VMEM((tb, w2), F32)
    return pl.pallas_call(
        functools.partial(_hgrn_kernel, layer=layer),
        grid=(batch, nt, npair),
        in_specs=[
            main(0), main(1), main(2), main(3),
            pl.BlockSpec((lb_logits.shape[0], w2), lambda b, t, h: (0, h)),
            pl.BlockSpec((1, w2), lambda b, t, h: (0, h)),
            pl.BlockSpec(sel.shape, lambda b, t, h: (0, 0)),
            pl.BlockSpec(masks.shape, lambda b, t, h: (0, 0, 0)),
        ],
        out_specs=pl.BlockSpec((tb, w2), lambda b, t, h: (b * nt + t, h)),
        out_shape=jax.ShapeDtypeStruct((n, width), BF16),
        scratch_shapes=[pltpu.VMEM((npair, 2, HG_HEAD, HG_HEAD), F32)] + [blk() for _ in range(5)],
        compiler_params=_cparams(3),
        name="hgrn",
    )(proj, proj, proj, proj, lb_logits, norm_w, sel, masks)


def _pad_cols(w, to):
    return jnp.pad(w, ((0, 0), (0, to - w.shape[1])))


def _pad_rows(w, to):
    return jnp.pad(w, ((0, to - w.shape[0]), (0, 0)))


def kernel(x, ffn1_norm, ffn1_w_gate, ffn1_w_up, ffn1_w_down, mix_norm, w_in,
           rwkv_mu, rwkv_w0, rwkv_w2, rwkv_a0, rwkv_a2, rwkv_g2, rwkv_k_k, rwkv_k_a,
           rwkv_r_k, rwkv_ln_w, rwkv_ln_b, hgrn_lb_logits, hgrn_norm, w_out,
           ffn2_norm, ffn2_w_gate, ffn2_w_up, ffn2_w_down, final_norm):
    batch, seq, d = x.shape
    depth = ffn1_norm.shape[0]
    c = rwkv_w0.shape[1]
    ch = hgrn_norm.shape[1]
    dl, al, gl = rwkv_w2.shape[1], rwkv_a2.shape[1], rwkv_g2.shape[1]
    assert dl <= LANES and al <= LANES and gl <= 2 * LANES
    lora_col = 3 * c

    xf = x.reshape(batch * seq, d)
    row = lambda v: v.reshape(1, -1).astype(F32)
    for l in range(depth):
        xf = _ffn(xf, row(ffn1_norm[l]), ffn1_w_gate[l].astype(BF16), ffn1_w_up[l].astype(BF16),
                  ffn1_w_down[l].astype(BF16), row(final_norm), final_norm=False)

        wi = w_in[l]
        o = 3 * c
        secs = [wi[:, :o], _pad_cols(wi[:, o:o + dl], LANES), _pad_cols(wi[:, o + dl:o + dl + al], LANES),
                _pad_cols(wi[:, o + dl + al:o + dl + al + gl], 2 * LANES), wi[:, o + dl + al + gl:]]
        w_cat = jnp.concatenate(secs, axis=1).astype(BF16)
        mu = rwkv_mu[l].reshape(1, -1)
        mu_cat = jnp.concatenate([mu[:, :o], _pad_cols(mu[:, o:o + dl], LANES),
                                  _pad_cols(mu[:, o + dl:o + dl + al], LANES),
                                  _pad_cols(mu[:, o + dl + al:o + dl + al + gl], 2 * LANES)], axis=1)
        proj = _inproj(xf, row(mix_norm[l]), w_cat)

        y_r = _rwkv(proj, mu_cat, row(rwkv_w0[l]), row(rwkv_a0[l]), row(rwkv_k_k[l]), row(rwkv_k_a[l]),
                    row(rwkv_r_k[l]), row(rwkv_ln_w[l]), row(rwkv_ln_b[l]),
                    _pad_rows(rwkv_w2[l], LANES).astype(BF16), _pad_rows(rwkv_a2[l], LANES).astype(BF16),
                    _pad_rows(rwkv_g2[l], 2 * LANES).astype(BF16),
                    batch=batch, seq=seq, width=c, lora_col=lora_col)
        y_h = _hgrn(proj, hgrn_lb_logits.astype(F32), row(hgrn_norm[l]), layer=l,
                    batch=batch, seq=seq, width=ch, col0=lora_col + 4 * LANES)
        wo = w_out[l].astype(BF16)
        xf = _outproj(xf, y_r, y_h, wo[:c], wo[c:])

        xf = _ffn(xf, row(ffn2_norm[l]), ffn2_w_gate[l].astype(BF16), ffn2_w_up[l].astype(BF16),
                  ffn2_w_down[l].astype(BF16), row(final_norm), final_norm=(l == depth - 1))
    return xf.reshape(batch, seq, d)
```

```python
import functools

import numpy as np
import jax
import jax.numpy as jnp
from jax import lax
from jax.experimental import pallas as pl
from jax.experimental.pallas import tpu as pltpu

F32 = jnp.float32
BF16 = jnp.bfloat16

RMS_EPS = 1e-6
GN_EPS = 64e-5
FFN_RES = 0.5
KK_EPS = 1e-12

LANES = 128
RW_HEAD = 64
HG_HEAD = 128
CHUNK = 64
STACK = 2 * CHUNK
VMEM_LIMIT = 56 * 1024 * 1024

HG_LEVELS = (32, 16, 8, 4, 2, 1)


def _dot(a, b):
    return jnp.dot(a, b, preferred_element_type=F32)


def _dot_nt(a, b):
    return lax.dot_general(a, b, (((1,), (1,)), ((), ())), preferred_element_type=F32)


def _dot_tn(a, b):
    return lax.dot_general(a, b, (((0,), (0,)), ((), ())), preferred_element_type=F32)


def _bdot(a, b):
    return _dot(a.astype(BF16), b.astype(BF16))


def _split2(x):
    hi = x.astype(BF16)
    lo = (x - hi.astype(F32)).astype(BF16)
    return hi, lo


def _dot_sel(sel, x):
    hi, lo = _split2(x)
    return _dot(sel, hi) + _dot(sel, lo)


def _dot_sel_r(x, sel):
    hi, lo = _split2(x)
    return _dot(hi, sel) + _dot(lo, sel)


def _sigmoid(x):
    return 1.0 / (1.0 + jnp.exp(-x))


def _silu(x):
    return x * _sigmoid(x)


def _cparams(n_axes):
    return pltpu.CompilerParams(dimension_semantics=("arbitrary",) * n_axes,
                                vmem_limit_bytes=VMEM_LIMIT)


def _ffn_kernel(x_ref, g_ref, wg_ref, wu_ref, wd_ref, fn_ref, o_ref, h_ref, acc_ref, *, final_norm):
    j = pl.program_id(1)

    @pl.when(j == 0)
    def _():
        x = x_ref[...]
        inv = lax.rsqrt(jnp.mean(x * x, axis=-1, keepdims=True) + RMS_EPS)
        h_ref[...] = ((x * inv) * g_ref[...]).astype(BF16)
        acc_ref[...] = jnp.zeros_like(acc_ref)

    h = h_ref[...]
    gate = _dot(h, wg_ref[...])
    up = _dot(h, wu_ref[...])
    act = (_silu(gate) * up).astype(BF16)
    acc_ref[...] += _dot(act, wd_ref[...])

    @pl.when(j == pl.num_programs(1) - 1)
    def _():
        y = x_ref[...] + FFN_RES * acc_ref[...]
        if final_norm:
            inv = lax.rsqrt(jnp.mean(y * y, axis=-1, keepdims=True) + RMS_EPS)
            y = (y * inv) * fn_ref[...]
        o_ref[...] = y


def _ffn(x, gain, wg, wu, wd, fnorm, *, final_norm, tm=512, tf=512):
    n, d = x.shape
    f = wg.shape[1]
    assert n % tm == 0 and f % tf == 0
    return pl.pallas_call(
        functools.partial(_ffn_kernel, final_norm=final_norm),
        grid=(n // tm, f // tf),
        in_specs=[
            pl.BlockSpec((tm, d), lambda i, j: (i, 0)),
            pl.BlockSpec((1, d), lambda i, j: (0, 0)),
            pl.BlockSpec((d, tf), lambda i, j: (0, j)),
            pl.BlockSpec((d, tf), lambda i, j: (0, j)),
            pl.BlockSpec((tf, d), lambda i, j: (j, 0)),
            pl.BlockSpec((1, d), lambda i, j: (0, 0)),
        ],
        out_specs=pl.BlockSpec((tm, d), lambda i, j: (i, 0)),
        out_shape=jax.ShapeDtypeStruct((n, d), F32),
        scratch_shapes=[pltpu.VMEM((tm, d), BF16), pltpu.VMEM((tm, d), F32)],
        compiler_params=_cparams(2),
        name="ffn_final" if final_norm else "ffn",
    )(x, gain, wg, wu, wd, fnorm)


def _inproj_kernel(x_ref, g_ref, w_ref, o_ref, h_ref):
    @pl.when(pl.program_id(1) == 0)
    def _():
        x = x_ref[...]
        inv = lax.rsqrt(jnp.mean(x * x, axis=-1, keepdims=True) + RMS_EPS)
        h_ref[...] = ((x * inv) * g_ref[...]).astype(BF16)

    o_ref[...] = _dot(h_ref[...], w_ref[...])


def _inproj(x, gain, w, *, tm=512, tn=1536):
    n, d = x.shape
    c = w.shape[1]
    assert n % tm == 0 and c % tn == 0
    return pl.pallas_call(
        _inproj_kernel,
        grid=(n // tm, c // tn),
        in_specs=[
            pl.BlockSpec((tm, d), lambda i, j: (i, 0)),
            pl.BlockSpec((1, d), lambda i, j: (0, 0)),
            pl.BlockSpec((d, tn), lambda i, j: (0, j)),
        ],
        out_specs=pl.BlockSpec((tm, tn), lambda i, j: (i, j)),
        out_shape=jax.ShapeDtypeStruct((n, c), F32),
        scratch_shapes=[pltpu.VMEM((tm, d), BF16)],
        compiler_params=_cparams(2),
        name="inproj",
    )(x, gain, w)


def _outproj_kernel(x_ref, yr_ref, yh_ref, wr_ref, wh_ref, o_ref):
    o_ref[...] = x_ref[...] + _dot(yr_ref[...], wr_ref[...]) + _dot(yh_ref[...], wh_ref[...])


def _outproj(x, yr, yh, wr, wh, *, tm=512):
    n, d = x.shape
    cr, ch = yr.shape[1], yh.shape[1]
    assert n % tm == 0
    return pl.pallas_call(
        _outproj_kernel,
        grid=(n // tm,),
        in_specs=[
            pl.BlockSpec((tm, d), lambda i: (i, 0)),
            pl.BlockSpec((tm, cr), lambda i: (i, 0)),
            pl.BlockSpec((tm, ch), lambda i: (i, 0)),
            pl.BlockSpec((cr, d), lambda i: (0, 0)),
            pl.BlockSpec((ch, d), lambda i: (0, 0)),
        ],
        out_specs=pl.BlockSpec((tm, d), lambda i: (i, 0)),
        out_shape=jax.ShapeDtypeStruct((n, d), F32),
        compiler_params=_cparams(1),
        name="outproj",
    )(x, yr, yh, wr, wh)


def _rwkv_masks():
    i = np.arange(STACK)[:, None]
    j = np.arange(STACK)[None, :]
    same = (i // CHUNK) == (j // CHUNK)
    sl = same & (i > j)
    il = same & (i >= j)
    blk = lambda s: (i // s) == (j // s)
    d8 = sl & blk(8)
    c16 = sl & blk(16) & ~blk(8)
    c32 = sl & blk(32) & ~blk(16)
    c64 = sl & ~blk(32)
    return np.stack([sl, il, d8, c16, c32, c64]).astype(np.float32)


def _head_sum_matrix():
    i = np.arange(LANES)[:, None]
    j = np.arange(LANES)[None, :]
    return ((i // RW_HEAD) == (j // RW_HEAD)).astype(np.float32)


def _tri64():
    i = np.arange(CHUNK)[:, None]
    j = np.arange(CHUNK)[None, :]
    return (i >= j).astype(np.float32)


def _hgrn_consts():
    i = np.arange(STACK)[:, None]
    j = np.arange(STACK)[None, :]
    same = (i // CHUNK) == (j // CHUNK)
    sels = [same & (i >= j)]
    masks = []
    for h in HG_LEVELS:
        m = (i // (2 * h)) * (2 * h) + h - 1
        sels.append(same & (j <= m))
        masks.append(((i // (2 * h)) == (j // (2 * h))) & ((i % (2 * h)) >= h) & ((j % (2 * h)) < h))
    masks.append(i == j)
    return (np.concatenate(sels, axis=0).astype(np.float32), np.stack(masks).astype(np.float32))


def _shift(x, prev8, mu, first):
    rows = lax.broadcasted_iota(jnp.int32, x.shape, 0)
    prev_row = jnp.where(first, 0.0, prev8[7:8, :])
    xp = jnp.where(rows == 0, prev_row, pltpu.roll(x, shift=1, axis=0))
    return x + (xp - x) * mu


def _stack2(x):
    lane = lax.broadcasted_iota(jnp.int32, x.shape, 1)
    lo = lane < RW_HEAD
    return jnp.concatenate([jnp.where(lo, x, 0.0), jnp.where(lo, 0.0, x)], axis=0)


def _unit_lower_inverse(ms, mk_ref):
    shape = ms[0].shape
    eye = (lax.broadcasted_iota(jnp.int32, shape, 0) == lax.broadcasted_iota(jnp.int32, shape, 1)).astype(F32)
    both = lambda f, xs, ys: [f(x, y) for x, y in zip(xs, ys)]
    m8 = [m * mk_ref[2] for m in ms]
    m8_2 = both(_bdot, m8, m8)
    ts = [eye + m for m in m8]
    ts = [t + d for t, d in zip(ts, both(_bdot, ts, m8_2))]
    m8_4 = both(_bdot, m8_2, m8_2)
    ts = [t + d for t, d in zip(ts, both(_bdot, ts, m8_4))]
    for lvl in (3, 4, 5):
        tc = both(_bdot, ts, [m * mk_ref[lvl] for m in ms])
        ts = [t + d for t, d in zip(ts, both(_bdot, tc, ts))]
    return ts


def _rwkv_kernel(r_ref, k_ref, v_ref, lo_ref, rp_ref, kp_ref, vp_ref, lop_ref,
                 mur_ref, muk_ref, muv_ref, mul_ref,
                 w0_ref, a0_ref, kk_ref, ka_ref, rk_ref, lnw_ref, lnb_ref,
                 w2_ref, a2_ref, g2_ref, tri_ref, e2_ref, mk_ref,
                 y_ref,
                 s_ref, r_s, k_s, v_s, al_s, be_s, lw_s, y_s):
    t = pl.program_id(1)
    p = pl.program_id(2)
    first = t == 0
    tb = r_ref.shape[0]
    nch = tb // CHUNK

    @pl.when(first)
    def _():
        s_ref[p] = jnp.zeros((LANES, LANES), F32)

    e2 = e2_ref[...]
    r = _shift(r_ref[...], rp_ref[...], mur_ref[...], first)
    k = _shift(k_ref[...], kp_ref[...], muk_ref[...], first)
    v = _shift(v_ref[...], vp_ref[...], muv_ref[...], first)
    lora = _shift(lo_ref[...], lop_ref[...], mul_ref[...], first)
    xw = lora[:, 0:LANES]
    xa = lora[:, LANES:2 * LANES]
    xg = lora[:, 2 * LANES:4 * LANES]

    z = w0_ref[...] + _dot(jnp.tanh(xw).astype(BF16), w2_ref[...])
    w = -(jnp.maximum(-z, 0.0) + jnp.log(1.0 + jnp.exp(-jnp.abs(z)))) - 0.5
    lw = -jnp.exp(w)
    a = _sigmoid(a0_ref[...] + _dot(xa.astype(BF16), a2_ref[...]))
    g = _dot(_sigmoid(xg).astype(BF16), g2_ref[...])

    kk = k * kk_ref[...]
    nrm = jnp.sqrt(_dot_sel_r(kk * kk, e2))
    kk = kk / jnp.maximum(nrm, KK_EPS)
    kmod = k * (1.0 + (a - 1.0) * ka_ref[...])
    bonus = _dot_sel_r(r * kmod * rk_ref[...], e2) * v

    r_s[...] = r
    k_s[...] = kmod
    v_s[...] = v
    al_s[...] = -kk
    be_s[...] = kk * a
    lw_s[...] = lw

    tri = tri_ref[...]

    cs = range(nch)
    bf = lambda xs: [x.astype(BF16) for x in xs]
    cat0 = lambda xs, ys: [jnp.concatenate([x, y], axis=0) for x, y in zip(xs, ys)]
    cat1 = lambda xs, ys: [jnp.concatenate([x, y], axis=1) for x, y in zip(xs, ys)]
    rows = [pl.ds(c * CHUNK, CHUNK) for c in cs]
    lwc = [lw_s[rw, :] for rw in rows]
    gc = [_dot_sel(tri, x) for x in lwc]
    glast = [g_[CHUNK - 1:CHUNK, :] for g_ in gc]
    e_in = [jnp.exp(g_) for g_ in gc]
    e_ex = [jnp.exp(g_ - l_) for g_, l_ in zip(gc, lwc)]
    e_inv = [jnp.exp(-g_) for g_ in gc]
    e_end = [jnp.exp(gl - g_) for gl, g_ in zip(glast, gc)]
    at = [_stack2(al_s[rw, :] * e) for rw, e in zip(rows, e_ex)]
    rt = [_stack2(r_s[rw, :] * e) for rw, e in zip(rows, e_in)]
    bt = [_stack2(be_s[rw, :] * e) for rw, e in zip(rows, e_inv)]
    kt = [_stack2(k_s[rw, :] * e) for rw, e in zip(rows, e_inv)]
    bh = [_stack2(be_s[rw, :] * e) for rw, e in zip(rows, e_end)]
    kh = [_stack2(k_s[rw, :] * e) for rw, e in zip(rows, e_end)]
    vs = [_stack2(v_s[rw, :]) for rw in rows]

    g1 = [_dot_nt(x, y) for x, y in zip(bf(cat0(at, rt)), bf(cat0(bt, kt)))]
    m_ab = [g_[0:STACK, 0:STACK] * mk_ref[0] for g_ in g1]
    m_ak = [g_[0:STACK, STACK:2 * STACK] * mk_ref[0] for g_ in g1]
    m_rb = [g_[STACK:2 * STACK, 0:STACK] * mk_ref[1] for g_ in g1]
    m_rk = [g_[STACK:2 * STACK, STACK:2 * STACK] * mk_ref[1] for g_ in g1]

    tinv = _unit_lower_inverse(m_ab, mk_ref)
    x = [_bdot(m, v_) for m, v_ in zip(m_ak, vs)]
    tx = [_bdot(t_, z_) for t_, z_ in zip(tinv, cat1(x, at))]
    u_loc = [t_[:, 0:LANES] for t_ in tx]
    wm = [t_[:, LANES:2 * LANES] for t_ in tx]
    y_loc = [_bdot(m, z_) for m, z_ in zip(cat1(m_rb, m_rk), cat0(u_loc, vs))]
    q = [r_ + _bdot(m, w_) for r_, m, w_ in zip(rt, m_rb, wm)]

    a_c = [_dot_tn(w_, b_) for w_, b_ in zip(bf(wm), bf(bh))]
    b_c = [_dot_tn(z_, w_) for z_, w_ in zip(bf(cat0(u_loc, vs)), bf(cat0(bh, kh)))]
    s = s_ref[p]
    for c in cs:
        y = y_loc[c] + _dot_nt(q[c].astype(BF16), s.astype(BF16))
        y_s[rows[c], :] = y[0:CHUNK] + y[CHUNK:STACK]
        s = s * jnp.exp(glast[c]) + _bdot(s, a_c[c]) + b_c[c]
    s_ref[p] = s

    y = y_s[...]
    inv_n = 1.0 / RW_HEAD
    mean = _dot_sel_r(y, e2) * inv_n
    d = y - mean
    var = _dot_sel_r(d * d, e2) * inv_n
    yn = d * lax.rsqrt(var + GN_EPS) * lnw_ref[...] + lnb_ref[...]
    y_ref[...] = ((yn + bonus) * g).astype(y_ref.dtype)


def _rwkv(proj, mu, w0, a0, k_k, k_a, r_k, ln_w, ln_b, w2p, a2p, g2p, *, batch, seq, width, lora_col, tb=512):
    n = proj.shape[0]
    nt = seq // tb
    npair = width // LANES
    wb = width // LANES
    lb4 = lora_col // (4 * LANES)
    rb8 = tb // 8

    def main(sec):
        return pl.BlockSpec((tb, LANES), lambda b, t, p: (b * nt + t, sec * wb + p))

    def prev(sec):
        return pl.BlockSpec((8, LANES), lambda b, t, p: (jnp.maximum((b * nt + t) * rb8 - 1, 0), sec * wb + p))

    def vec(sec):
        return pl.BlockSpec((1, LANES), lambda b, t, p: (0, sec * wb + p))

    pvec = pl.BlockSpec((1, LANES), lambda b, t, p: (0, p))
    const2 = lambda shape: pl.BlockSpec(shape, lambda b, t, p: (0, 0))
    tri = jnp.asarray(_tri64(), BF16)
    e2 = jnp.asarray(_head_sum_matrix(), BF16)
    masks = jnp.asarray(_rwkv_masks(), F32)
    blk = lambda: pltpu.VMEM((tb, LANES), F32)
    return pl.pallas_call(
        _rwkv_kernel,
        grid=(batch, nt, npair),
        in_specs=[
            main(0), main(1), main(2),
            pl.BlockSpec((tb, 4 * LANES), lambda b, t, p: (b * nt + t, lb4)),
            prev(0), prev(1), prev(2),
            pl.BlockSpec((8, 4 * LANES), lambda b, t, p: (jnp.maximum((b * nt + t) * rb8 - 1, 0), lb4)),
            vec(0), vec(1), vec(2),
            pl.BlockSpec((1, 4 * LANES), lambda b, t, p: (0, lb4)),
            pvec, pvec, pvec, pvec, pvec, pvec, pvec,
            pl.BlockSpec((LANES, LANES), lambda b, t, p: (0, p)),
            pl.BlockSpec((LANES, LANES), lambda b, t, p: (0, p)),
            pl.BlockSpec((2 * LANES, LANES), lambda b, t, p: (0, p)),
            const2((CHUNK, CHUNK)), const2((LANES, LANES)),
            pl.BlockSpec((6, STACK, STACK), lambda b, t, p: (0, 0, 0)),
        ],
        out_specs=pl.BlockSpec((tb, LANES), lambda b, t, p: (b * nt + t, p)),
        out_shape=jax.ShapeDtypeStruct((n, width), BF16),
        scratch_shapes=[pltpu.VMEM((npair, LANES, LANES), F32)] + [blk() for _ in range(7)],
        compiler_params=_cparams(3),
        name="rwkv",
    )(proj, proj, proj, proj, proj, proj, proj, proj, mu, mu, mu, mu,
      w0, a0, k_k, k_a, r_k, ln_w, ln_b, w2p, a2p, g2p, tri, e2, masks)


def _hgrn_kernel(q_ref, f_ref, i_ref, g_ref, lg_ref, nw_ref, sel_ref, mk_ref,
                 y_ref,
                 s_ref, q_s, k_s, v_s, lf_s, o_s, *, layer):
    t = pl.program_id(1)
    hp = pl.program_id(2)
    tb = q_ref.shape[0]
    nch = tb // CHUNK
    nlev = len(HG_LEVELS)

    @pl.when(t == 0)
    def _():
        s_ref[hp] = jnp.zeros((2, HG_HEAD, HG_HEAD), F32)

    lg = lg_ref[...]
    mx = jnp.max(lg, axis=0, keepdims=True)
    ex = jnp.exp(lg - mx)
    lb = jnp.sum(ex[0:layer + 1, :], axis=0, keepdims=True) / jnp.sum(ex, axis=0, keepdims=True)
    forget = lb + (1.0 - lb) * _sigmoid(f_ref[...])
    q_s[...] = _silu(q_ref[...])
    k_s[...] = 1.0 - forget
    v_s[...] = i_ref[...]
    lf_s[...] = jnp.log(forget)

    def rows2(ref, rows):
        blk = ref[rows, :]
        return jnp.concatenate([blk[:, 0:HG_HEAD], blk[:, HG_HEAD:2 * HG_HEAD]], axis=0)

    cs = range(nch)
    rows = [pl.ds(c * CHUNK, CHUNK) for c in cs]
    qc = [rows2(q_s, rw) for rw in rows]
    kc = [rows2(k_s, rw) for rw in rows]
    vc = [rows2(v_s, rw) for rw in rows]
    sel = sel_ref[...]
    gall = [_dot_sel(sel, rows2(lf_s, rw)) for rw in rows]
    gc = [g_[0:STACK] for g_ in gall]
    a = [_dot_nt(q_.astype(BF16), k_.astype(BF16)) * mk_ref[nlev] for q_, k_ in zip(qc, kc)]
    for li in range(nlev):
        gref = [g_[(li + 1) * STACK:(li + 2) * STACK] for g_ in gall]
        qh = [(q_ * jnp.exp(jnp.minimum(g_ - r_, 0.0))).astype(BF16) for q_, g_, r_ in zip(qc, gc, gref)]
        kh = [(k_ * jnp.exp(jnp.minimum(r_ - g_, 0.0))).astype(BF16) for k_, g_, r_ in zip(kc, gc, gref)]
        a = [a_ + _dot_nt(q_, k_) * mk_ref[li] for a_, q_, k_ in zip(a, qh, kh)]
    o = [_bdot(a_, v_) for a_, v_ in zip(a, vc)]
    qg = [(q_ * jnp.exp(g_)).astype(BF16) for q_, g_ in zip(qc, gc)]
    s_pair = s_ref[hp]
    halves = []
    new = []
    for h in range(2):
        sl = slice(h * CHUNK, (h + 1) * CHUNK)
        glast = [g_[(h + 1) * CHUNK - 1:(h + 1) * CHUNK, :] for g_ in gc]
        khat = [(k_[sl] * jnp.exp(gl - g_[sl])).astype(BF16) for k_, gl, g_ in zip(kc, glast, gc)]
        delta = [_dot_tn(v_[sl].astype(BF16), k_) for v_, k_ in zip(vc, khat)]
        s = s_pair[h]
        s_in = []
        for c in cs:
            s_in.append(s.astype(BF16))
            s = s * jnp.exp(glast[c]) + delta[c]
        new.append(s)
        halves.append([o_[sl] + _dot_nt(q_[sl], s_) for o_, q_, s_ in zip(o, qg, s_in)])
    s_ref[hp] = jnp.stack(new)
    for c in cs:
        o_s[rows[c], :] = jnp.concatenate([halves[0][c], halves[1][c]], axis=1)

    o = o_s[...]
    gate = _silu(g_ref[...])
    nw = nw_ref[...]
    parts = []
    for h in range(2):
        oh = o[:, h * HG_HEAD:(h + 1) * HG_HEAD]
        inv = lax.rsqrt(jnp.mean(oh * oh, axis=-1, keepdims=True) + RMS_EPS)
        parts.append(oh * inv)
    y_ref[...] = (jnp.concatenate(parts, axis=1) * nw * gate).astype(y_ref.dtype)


def _hgrn(proj, lb_logits, norm_w, *, layer, batch, seq, width, col0, tb=512):
    n = proj.shape[0]
    nt = seq // tb
    w2 = 2 * HG_HEAD
    npair = width // w2
    cb = col0 // w2
    sec = width // w2
    sel_np, mk_np = _hgrn_consts()
    sel = jnp.asarray(sel_np, BF16)
    masks = jnp.asarray(mk_np, F32)

    def main(s):
        return pl.BlockSpec((tb, w2), lambda b, t, h: (b * nt + t, cb + s * sec + h))

    blk = lambda: pltpu.VMEM((tb, w2), F32)
    return pl.pallas_call(
        functools.partial(_hgrn_kernel, layer=layer),
        grid=(batch, nt, npair),
        in_specs=[
            main(0), main(1), main(2), main(3),
            pl.BlockSpec((lb_logits.shape[0], w2), lambda b, t, h: (0, h)),
            pl.BlockSpec((1, w2), lambda b, t, h: (0, h)),
            pl.BlockSpec(sel.shape, lambda b, t, h: (0, 0)),
            pl.BlockSpec(masks.shape, lambda b, t, h: (0, 0, 0)),
        ],
        out_specs=pl.BlockSpec((tb, w2), lambda b, t, h: (b * nt + t, h)),
        out_shape=jax.ShapeDtypeStruct((n, width), BF16),
        scratch_shapes=[pltpu.VMEM((npair, 2, HG_HEAD, HG_HEAD), F32)] + [blk() for _ in range(5)],
        compiler_params=_cparams(3),
        name="hgrn",
    )(proj, proj, proj, proj, lb_logits, norm_w, sel, masks)


def _pad_cols(w, to):
    return jnp.pad(w, ((0, 0), (0, to - w.shape[1])))


def _pad_rows(w, to):
    return jnp.pad(w, ((0, to - w.shape[0]), (0, 0)))


def kernel(x, ffn1_norm, ffn1_w_gate, ffn1_w_up, ffn1_w_down, mix_norm, w_in,
           rwkv_mu, rwkv_w0, rwkv_w2, rwkv_a0, rwkv_a2, rwkv_g2, rwkv_k_k, rwkv_k_a,
           rwkv_r_k, rwkv_ln_w, rwkv_ln_b, hgrn_lb_logits, hgrn_norm, w_out,
           ffn2_norm, ffn2_w_gate, ffn2_w_up, ffn2_w_down, final_norm):
    batch, seq, d = x.shape
    depth = ffn1_norm.shape[0]
    c = rwkv_w0.shape[1]
    ch = hgrn_norm.shape[1]
    dl, al, gl = rwkv_w2.shape[1], rwkv_a2.shape[1], rwkv_g2.shape[1]
    assert dl <= LANES and al <= LANES and gl <= 2 * LANES
    lora_col = 3 * c

    xf = x.reshape(batch * seq, d)
    row = lambda v: v.reshape(1, -1).astype(F32)
    for l in range(depth):
        xf = _ffn(xf, row(ffn1_norm[l]), ffn1_w_gate[l].astype(BF16), ffn1_w_up[l].astype(BF16),
                  ffn1_w_down[l].astype(BF16), row(final_norm), final_norm=False)

        wi = w_in[l]
        o = 3 * c
        secs = [wi[:, :o], _pad_cols(wi[:, o:o + dl], LANES), _pad_cols(wi[:, o + dl:o + dl + al], LANES),
                _pad_cols(wi[:, o + dl + al:o + dl + al + gl], 2 * LANES), wi[:, o + dl + al + gl:]]
        w_cat = jnp.concatenate(secs, axis=1).astype(BF16)
        mu = rwkv_mu[l].reshape(1, -1)
        mu_cat = jnp.concatenate([mu[:, :o], _pad_cols(mu[:, o:o + dl], LANES),
                                  _pad_cols(mu[:, o + dl:o + dl + al], LANES),
                                  _pad_cols(mu[:, o + dl + al:o + dl + al + gl], 2 * LANES)], axis=1)
        proj = _inproj(xf, row(mix_norm[l]), w_cat)

        y_r = _rwkv(proj, mu_cat, row(rwkv_w0[l]), row(rwkv_a0[l]), row(rwkv_k_k[l]), row(rwkv_k_a[l]),
                    row(rwkv_r_k[l]), row(rwkv_ln_w[l]), row(rwkv_ln_b[l]),
                    _pad_rows(rwkv_w2[l], LANES).astype(BF16), _pad_rows(rwkv_a2[l], LANES).astype(BF16),
                    _pad_rows(rwkv_g2[l], 2 * LANES).astype(BF16),
                    batch=batch, seq=seq, width=c, lora_col=lora_col)
        y_h = _hgrn(proj, hgrn_lb_logits.astype(F32), row(hgrn_norm[l]), layer=l,
                    batch=batch, seq=seq, width=ch, col0=lora_col + 4 * LANES)
        wo = w_out[l].astype(BF16)
        xf = _outproj(xf, y_r, y_h, wo[:c], wo[c:])

        xf = _ffn(xf, row(ffn2_norm[l]), ffn2_w_gate[l].astype(BF16), ffn2_w_up[l].astype(BF16),
                  ffn2_w_down[l].astype(BF16), row(final_norm), final_norm=(l == depth - 1))
    return xf.reshape(batch, seq, d)
```

```python
import functools

import numpy as np
import jax
import jax.numpy as jnp
from jax import lax
from jax.experimental import pallas as pl
from jax.experimental.pallas import tpu as pltpu

F32 = jnp.float32
BF16 = jnp.bfloat16

RMS_EPS = 1e-6
GN_EPS = 64e-5
FFN_RES = 0.5
KK_EPS = 1e-12

LANES = 128
RW_HEAD = 64
HG_HEAD = 128
CHUNK = 64
STACK = 2 * CHUNK
VMEM_LIMIT = 56 * 1024 * 1024

HG_LEVELS = (32, 16, 8, 4, 2, 1)


def _dot(a, b):
    return jnp.dot(a, b, preferred_element_type=F32)


def _dot_nt(a, b):
    return lax.dot_general(a, b, (((1,), (1,)), ((), ())), preferred_element_type=F32)


def _dot_tn(a, b):
    return lax.dot_general(a, b, (((0,), (0,)), ((), ())), preferred_element_type=F32)


def _bdot(a, b):
    return _dot(a.astype(BF16), b.astype(BF16))


def _split2(x):
    hi = x.astype(BF16)
    lo = (x - hi.astype(F32)).astype(BF16)
    return hi, lo


def _dot_sel(sel, x):
    hi, lo = _split2(x)
    return _dot(sel, hi) + _dot(sel, lo)


def _dot_sel_r(x, sel):
    hi, lo = _split2(x)
    return _dot(hi, sel) + _dot(lo, sel)


def _sigmoid(x):
    return 1.0 / (1.0 + jnp.exp(-x))


def _silu(x):
    return x * _sigmoid(x)


def _cparams(n_axes):
    return pltpu.CompilerParams(dimension_semantics=("arbitrary",) * n_axes,
                                vmem_limit_bytes=VMEM_LIMIT)


def _ffn_kernel(x_ref, g_ref, wg_ref, wu_ref, wd_ref, fn_ref, o_ref, h_ref, acc_ref, *, final_norm):
    j = pl.program_id(1)

    @pl.when(j == 0)
    def _():
        x = x_ref[...]
        inv = lax.rsqrt(jnp.mean(x * x, axis=-1, keepdims=True) + RMS_EPS)
        h_ref[...] = ((x * inv) * g_ref[...]).astype(BF16)
        acc_ref[...] = jnp.zeros_like(acc_ref)

    h = h_ref[...]
    gate = _dot(h, wg_ref[...])
    up = _dot(h, wu_ref[...])
    act = (_silu(gate) * up).astype(BF16)
    acc_ref[...] += _dot(act, wd_ref[...])

    @pl.when(j == pl.num_programs(1) - 1)
    def _():
        y = x_ref[...] + FFN_RES * acc_ref[...]
        if final_norm:
            inv = lax.rsqrt(jnp.mean(y * y, axis=-1, keepdims=True) + RMS_EPS)
            y = (y * inv) * fn_ref[...]
        o_ref[...] = y


def _ffn(x, gain, wg, wu, wd, fnorm, *, final_norm, tm=512, tf=512):
    n, d = x.shape
    f = wg.shape[1]
    assert n % tm == 0 and f % tf == 0
    return pl.pallas_call(
        functools.partial(_ffn_kernel, final_norm=final_norm),
        grid=(n // tm, f // tf),
        in_specs=[
            pl.BlockSpec((tm, d), lambda i, j: (i, 0)),
            pl.BlockSpec((1, d), lambda i, j: (0, 0)),
            pl.BlockSpec((d, tf), lambda i, j: (0, j)),
            pl.BlockSpec((d, tf), lambda i, j: (0, j)),
            pl.BlockSpec((tf, d), lambda i, j: (j, 0)),
            pl.BlockSpec((1, d), lambda i, j: (0, 0)),
        ],
        out_specs=pl.BlockSpec((tm, d), lambda i, j: (i, 0)),
        out_shape=jax.ShapeDtypeStruct((n, d), F32),
        scratch_shapes=[pltpu.VMEM((tm, d), BF16), pltpu.VMEM((tm, d), F32)],
        compiler_params=_cparams(2),
        name="ffn_final" if final_norm else "ffn",
    )(x, gain, wg, wu, wd, fnorm)


def _inproj_kernel(x_ref, g_ref, w_ref, o_ref, h_ref):
    @pl.when(pl.program_id(1) == 0)
    def _():
        x = x_ref[...]
        inv = lax.rsqrt(jnp.mean(x * x, axis=-1, keepdims=True) + RMS_EPS)
        h_ref[...] = ((x * inv) * g_ref[...]).astype(BF16)

    o_ref[...] = _dot(h_ref[...], w_ref[...])


def _inproj(x, gain, w, *, tm=1024, tn=1536):
    n, d = x.shape
    c = w.shape[1]
    assert n % tm == 0 and c % tn == 0
    return pl.pallas_call(
        _inproj_kernel,
        grid=(n // tm, c // tn),
        in_specs=[
            pl.BlockSpec((tm, d), lambda i, j: (i, 0)),
            pl.BlockSpec((1, d), lambda i, j: (0, 0)),
            pl.BlockSpec((d, tn), lambda i, j: (0, j)),
        ],
        out_specs=pl.BlockSpec((tm, tn), lambda i, j: (i, j)),
        out_shape=jax.ShapeDtypeStruct((n, c), F32),
        scratch_shapes=[pltpu.VMEM((tm, d), BF16)],
        compiler_params=_cparams(2),
        name="inproj",
    )(x, gain, w)


def _outproj_kernel(x_ref, yr_ref, yh_ref, wr_ref, wh_ref, o_ref):
    o_ref[...] = x_ref[...] + _dot(yr_ref[...], wr_ref[...]) + _dot(yh_ref[...], wh_ref[...])


def _outproj(x, yr, yh, wr, wh, *, tm=512):
    n, d = x.shape
    cr, ch = yr.shape[1], yh.shape[1]
    assert n % tm == 0
    return pl.pallas_call(
        _outproj_kernel,
        grid=(n // tm,),
        in_specs=[
            pl.BlockSpec((tm, d), lambda i: (i, 0)),
            pl.BlockSpec((tm, cr), lambda i: (i, 0)),
            pl.BlockSpec((tm, ch), lambda i: (i, 0)),
            pl.BlockSpec((cr, d), lambda i: (0, 0)),
            pl.BlockSpec((ch, d), lambda i: (0, 0)),
        ],
        out_specs=pl.BlockSpec((tm, d), lambda i: (i, 0)),
        out_shape=jax.ShapeDtypeStruct((n, d), F32),
        compiler_params=_cparams(1),
        name="outproj",
    )(x, yr, yh, wr, wh)


def _rwkv_masks():
    i = np.arange(STACK)[:, None]
    j = np.arange(STACK)[None, :]
    same = (i // CHUNK) == (j // CHUNK)
    sl = same & (i > j)
    il = same & (i >= j)
    blk = lambda s: (i // s) == (j // s)
    d8 = sl & blk(8)
    c16 = sl & blk(16) & ~blk(8)
    c32 = sl & blk(32) & ~blk(16)
    c64 = sl & ~blk(32)
    return np.stack([sl, il, d8, c16, c32, c64]).astype(np.float32)


def _head_sum_matrix():
    i = np.arange(LANES)[:, None]
    j = np.arange(LANES)[None, :]
    return ((i // RW_HEAD) == (j // RW_HEAD)).astype(np.float32)


def _tri64():
    i = np.arange(CHUNK)[:, None]
    j = np.arange(CHUNK)[None, :]
    return (i >= j).astype(np.float32)


def _hgrn_consts():
    i = np.arange(STACK)[:, None]
    j = np.arange(STACK)[None, :]
    same = (i // CHUNK) == (j // CHUNK)
    sels = [same & (i >= j)]
    masks = []
    for h in HG_LEVELS:
        m = (i // (2 * h)) * (2 * h) + h - 1
        sels.append(same & (j <= m))
        masks.append(((i // (2 * h)) == (j // (2 * h))) & ((i % (2 * h)) >= h) & ((j % (2 * h)) < h))
    masks.append(i == j)
    return (np.concatenate(sels, axis=0).astype(np.float32), np.stack(masks).astype(np.float32))


def _shift(x, prev8, mu, first):
    rows = lax.broadcasted_iota(jnp.int32, x.shape, 0)
    prev_row = jnp.where(first, 0.0, prev8[7:8, :])
    xp = jnp.where(rows == 0, prev_row, pltpu.roll(x, shift=1, axis=0))
    return x + (xp - x) * mu


def _stack2(x):
    lane = lax.broadcasted_iota(jnp.int32, x.shape, 1)
    lo = lane < RW_HEAD
    return jnp.concatenate([jnp.where(lo, x, 0.0), jnp.where(lo, 0.0, x)], axis=0)


def _unit_lower_inverse(ms, mk_ref):
    shape = ms[0].shape
    eye = (lax.broadcasted_iota(jnp.int32, shape, 0) == lax.broadcasted_iota(jnp.int32, shape, 1)).astype(F32)
    both = lambda f, xs, ys: [f(x, y) for x, y in zip(xs, ys)]
    m8 = [m * mk_ref[2] for m in ms]
    m8_2 = both(_bdot, m8, m8)
    ts = [eye + m for m in m8]
    ts = [t + d for t, d in zip(ts, both(_bdot, ts, m8_2))]
    m8_4 = both(_bdot, m8_2, m8_2)
    ts = [t + d for t, d in zip(ts, both(_bdot, ts, m8_4))]
    for lvl in (3, 4, 5):
        tc = both(_bdot, ts, [m * mk_ref[lvl] for m in ms])
        ts = [t + d for t, d in zip(ts, both(_bdot, tc, ts))]
    return ts


def _rwkv_kernel(r_ref, k_ref, v_ref, lo_ref, rp_ref, kp_ref, vp_ref, lop_ref,
                 mur_ref, muk_ref, muv_ref, mul_ref,
                 w0_ref, a0_ref, kk_ref, ka_ref, rk_ref, lnw_ref, lnb_ref,
                 w2_ref, a2_ref, g2_ref, tri_ref, e2_ref, mk_ref,
                 y_ref,
                 s_ref, r_s, k_s, v_s, al_s, be_s, lw_s, y_s):
    t = pl.program_id(1)
    p = pl.program_id(2)
    first = t == 0
    tb = r_ref.shape[0]
    nch = tb // CHUNK
    ng = r_ref.shape[1] // LANES
    lanes = [slice(g * LANES, (g + 1) * LANES) for g in range(ng)]

    @pl.when(first)
    def _():
        for g in range(ng):
            s_ref[p * ng + g] = jnp.zeros((LANES, LANES), F32)

    e2 = e2_ref[...]
    head_sum = lambda x: jnp.concatenate([_dot_sel_r(x[:, ln], e2) for ln in lanes], axis=1)
    r = _shift(r_ref[...], rp_ref[...], mur_ref[...], first)
    k = _shift(k_ref[...], kp_ref[...], muk_ref[...], first)
    v = _shift(v_ref[...], vp_ref[...], muv_ref[...], first)
    lora = _shift(lo_ref[...], lop_ref[...], mul_ref[...], first)
    xw = lora[:, 0:LANES]
    xa = lora[:, LANES:2 * LANES]
    xg = lora[:, 2 * LANES:4 * LANES]

    z = w0_ref[...] + _dot(jnp.tanh(xw).astype(BF16), w2_ref[...])
    w = -(jnp.maximum(-z, 0.0) + jnp.log(1.0 + jnp.exp(-jnp.abs(z)))) - 0.5
    lw = -jnp.exp(w)
    a = _sigmoid(a0_ref[...] + _dot(xa.astype(BF16), a2_ref[...]))
    gate = _dot(_sigmoid(xg).astype(BF16), g2_ref[...])

    kk = k * kk_ref[...]
    nrm = jnp.sqrt(head_sum(kk * kk))
    kk = kk / jnp.maximum(nrm, KK_EPS)
    kmod = k * (1.0 + (a - 1.0) * ka_ref[...])
    bonus = head_sum(r * kmod * rk_ref[...]) * v

    r_s[...] = r
    k_s[...] = kmod
    v_s[...] = v
    al_s[...] = -kk
    be_s[...] = kk * a
    lw_s[...] = lw

    tri = tri_ref[...]

    bf = lambda xs: [x.astype(BF16) for x in xs]
    cat0 = lambda xs, ys: [jnp.concatenate([x, y], axis=0) for x, y in zip(xs, ys)]
    cat1 = lambda xs, ys: [jnp.concatenate([x, y], axis=1) for x, y in zip(xs, ys)]
    tiles = [(pl.ds(c * CHUNK, CHUNK), ln) for c in range(nch) for ln in lanes]
    lwc = [lw_s[rw, ln] for rw, ln in tiles]
    gc = [_dot_sel(tri, x) for x in lwc]
    glast = [g_[CHUNK - 1:CHUNK, :] for g_ in gc]
    e_in = [jnp.exp(g_) for g_ in gc]
    e_ex = [jnp.exp(g_ - l_) for g_, l_ in zip(gc, lwc)]
    e_inv = [jnp.exp(-g_) for g_ in gc]
    e_end = [jnp.exp(gl - g_) for gl, g_ in zip(glast, gc)]
    at = [_stack2(al_s[rw, ln] * e) for (rw, ln), e in zip(tiles, e_ex)]
    rt = [_stack2(r_s[rw, ln] * e) for (rw, ln), e in zip(tiles, e_in)]
    bt = [_stack2(be_s[rw, ln] * e) for (rw, ln), e in zip(tiles, e_inv)]
    kt = [_stack2(k_s[rw, ln] * e) for (rw, ln), e in zip(tiles, e_inv)]
    bh = [_stack2(be_s[rw, ln] * e) for (rw, ln), e in zip(tiles, e_end)]
    kh = [_stack2(k_s[rw, ln] * e) for (rw, ln), e in zip(tiles, e_end)]
    vs = [_stack2(v_s[rw, ln]) for rw, ln in tiles]

    g1 = [_dot_nt(x, y) for x, y in zip(bf(cat0(at, rt)), bf(cat0(bt, kt)))]
    m_ab = [g_[0:STACK, 0:STACK] * mk_ref[0] for g_ in g1]
    m_ak = [g_[0:STACK, STACK:2 * STACK] * mk_ref[0] for g_ in g1]
    m_rb = [g_[STACK:2 * STACK, 0:STACK] * mk_ref[1] for g_ in g1]
    m_rk = [g_[STACK:2 * STACK, STACK:2 * STACK] * mk_ref[1] for g_ in g1]

    tinv = _unit_lower_inverse(m_ab, mk_ref)
    x = [_bdot(m, v_) for m, v_ in zip(m_ak, vs)]
    tx = [_bdot(t_, z_) for t_, z_ in zip(tinv, cat1(x, at))]
    u_loc = [t_[:, 0:LANES] for t_ in tx]
    wm = [t_[:, LANES:2 * LANES] for t_ in tx]
    y_loc = [_bdot(m, z_) for m, z_ in zip(cat1(m_rb, m_rk), cat0(u_loc, vs))]
    q = [r_ + _bdot(m, w_) for r_, m, w_ in zip(rt, m_rb, wm)]

    a_c = [_dot_tn(w_, b_) for w_, b_ in zip(bf(wm), bf(bh))]
    b_c = [_dot_tn(z_, w_) for z_, w_ in zip(bf(cat0(u_loc, vs)), bf(cat0(bh, kh)))]
    ss = [s_ref[p * ng + g] for g in range(ng)]
    for i, (rw, ln) in enumerate(tiles):
        g = i % ng
        s = ss[g]
        y = y_loc[i] + _dot_nt(q[i].astype(BF16), s.astype(BF16))
        y_s[rw, ln] = y[0:CHUNK] + y[CHUNK:STACK]
        ss[g] = s * jnp.exp(glast[i]) + _bdot(s, a_c[i]) + b_c[i]
    for g in range(ng):
        s_ref[p * ng + g] = ss[g]

    y = y_s[...]
    inv_n = 1.0 / RW_HEAD
    mean = head_sum(y) * inv_n
    d = y - mean
    var = head_sum(d * d) * inv_n
    yn = d * lax.rsqrt(var + GN_EPS) * lnw_ref[...] + lnb_ref[...]
    y_ref[...] = ((yn + bonus) * gate).astype(y_ref.dtype)


def _rwkv(proj, mu, w0, a0, k_k, k_a, r_k, ln_w, ln_b, w2p, a2p, g2p, *, batch, seq, width, lora_col,
          tb=512, pairs_per_step=4):
    n = proj.shape[0]
    nt = seq // tb
    npair = width // LANES
    cw = pairs_per_step * LANES
    wb = width // cw
    lb4 = lora_col // (4 * LANES)
    rb8 = tb // 8
    assert npair % pairs_per_step == 0 and seq % tb == 0

    def main(sec):
        return pl.BlockSpec((tb, cw), lambda b, t, p: (b * nt + t, sec * wb + p))

    def prev(sec):
        return pl.BlockSpec((8, cw), lambda b, t, p: (jnp.maximum((b * nt + t) * rb8 - 1, 0), sec * wb + p))

    def vec(sec):
        return pl.BlockSpec((1, cw), lambda b, t, p: (0, sec * wb + p))

    pvec = pl.BlockSpec((1, cw), lambda b, t, p: (0, p))
    const2 = lambda shape: pl.BlockSpec(shape, lambda b, t, p: (0, 0))
    tri = jnp.asarray(_tri64(), BF16)
    e2 = jnp.asarray(_head_sum_matrix(), BF16)
    masks = jnp.asarray(_rwkv_masks(), F32)
    blk = lambda: pltpu.VMEM((tb, cw), F32)
    return pl.pallas_call(
        _rwkv_kernel,
        grid=(batch, nt, wb),
        in_specs=[
            main(0), main(1), main(2),
            pl.BlockSpec((tb, 4 * LANES), lambda b, t, p: (b * nt + t, lb4)),
            prev(0), prev(1), prev(2),
            pl.BlockSpec((8, 4 * LANES), lambda b, t, p: (jnp.maximum((b * nt + t) * rb8 - 1, 0), lb4)),
            vec(0), vec(1), vec(2),
            pl.BlockSpec((1, 4 * LANES), lambda b, t, p: (0, lb4)),
            pvec, pvec, pvec, pvec, pvec, pvec, pvec,
            pl.BlockSpec((LANES, cw), lambda b, t, p: (0, p)),
            pl.BlockSpec((LANES, cw), lambda b, t, p: (0, p)),
            pl.BlockSpec((2 * LANES, cw), lambda b, t, p: (0, p)),
            const2((CHUNK, CHUNK)), const2((LANES, LANES)),
            pl.BlockSpec((6, STACK, STACK), lambda b, t, p: (0, 0, 0)),
        ],
        out_specs=pl.BlockSpec((tb, cw), lambda b, t, p: (b * nt + t, p)),
        out_shape=jax.ShapeDtypeStruct((n, width), BF16),
        scratch_shapes=[pltpu.VMEM((npair, LANES, LANES), F32)] + [blk() for _ in range(7)],
        compiler_params=_cparams(3),
        name="rwkv",
    )(proj, proj, proj, proj, proj, proj, proj, proj, mu, mu, mu, mu,
      w0, a0, k_k, k_a, r_k, ln_w, ln_b, w2p, a2p, g2p, tri, e2, masks)


def _hgrn_kernel(q_ref, f_ref, i_ref, g_ref, lg_ref, nw_ref, sel_ref, mk_ref,
                 y_ref,
                 s_ref, q_s, k_s, v_s, lf_s, o_s, *, layer):
    t = pl.program_id(1)
    hp = pl.program_id(2)
    tb = q_ref.shape[0]
    nch = tb // CHUNK
    nlev = len(HG_LEVELS)

    @pl.when(t == 0)
    def _():
        s_ref[hp] = jnp.zeros((2, HG_HEAD, HG_HEAD), F32)

    lg = lg_ref[...]
    mx = jnp.max(lg, axis=0, keepdims=True)
    ex = jnp.exp(lg - mx)
    lb = jnp.sum(ex[0:layer + 1, :], axis=0, keepdims=True) / jnp.sum(ex, axis=0, keepdims=True)
    forget = lb + (1.0 - lb) * _sigmoid(f_ref[...])
    q_s[...] = _silu(q_ref[...])
    k_s[...] = 1.0 - forget
    v_s[...] = i_ref[...]
    lf_s[...] = jnp.log(forget)

    def rows2(ref, rows):
        blk = ref[rows, :]
        return jnp.concatenate([blk[:, 0:HG_HEAD], blk[:, HG_HEAD:2 * HG_HEAD]], axis=0)

    cs = range(nch)
    rows = [pl.ds(c * CHUNK, CHUNK) for c in cs]
    qc = [rows2(q_s, rw) for rw in rows]
    kc = [rows2(k_s, rw) for rw in rows]
    vc = [rows2(v_s, rw) for rw in rows]
    sel = sel_ref[...]
    gall = [_dot_sel(sel, rows2(lf_s, rw)) for rw in rows]
    gc = [g_[0:STACK] for g_ in gall]
    a = [_dot_nt(q_.astype(BF16), k_.astype(BF16)) * mk_ref[nlev] for q_, k_ in zip(qc, kc)]
    for li in range(nlev):
        gref = [g_[(li + 1) * STACK:(li + 2) * STACK] for g_ in gall]
        ed = [jnp.exp(-jnp.abs(g_ - r_)) for g_, r_ in zip(gc, gref)]
        qh = [(q_ * e).astype(BF16) for q_, e in zip(qc, ed)]
        kh = [(k_ * e).astype(BF16) for k_, e in zip(kc, ed)]
        a = [a_ + _dot_nt(q_, k_) * mk_ref[li] for a_, q_, k_ in zip(a, qh, kh)]
    o = [_bdot(a_, v_) for a_, v_ in zip(a, vc)]
    qg = [(q_ * jnp.exp(g_)).astype(BF16) for q_, g_ in zip(qc, gc)]
    s_pair = s_ref[hp]
    halves = []
    new = []
    for h in range(2):
        sl = slice(h * CHUNK, (h + 1) * CHUNK)
        glast = [g_[(h + 1) * CHUNK - 1:(h + 1) * CHUNK, :] for g_ in gc]
        khat = [(k_[sl] * jnp.exp(gl - g_[sl])).astype(BF16) for k_, gl, g_ in zip(kc, glast, gc)]
        delta = [_dot_tn(v_[sl].astype(BF16), k_) for v_, k_ in zip(vc, khat)]
        s = s_pair[h]
        s_in = []
        for c in cs:
            s_in.append(s.astype(BF16))
            s = s * jnp.exp(glast[c]) + delta[c]
        new.append(s)
        halves.append([o_[sl] + _dot_nt(q_[sl], s_) for o_, q_, s_ in zip(o, qg, s_in)])
    s_ref[hp] = jnp.stack(new)
    for c in cs:
        o_s[rows[c], :] = jnp.concatenate([halves[0][c], halves[1][c]], axis=1)

    o = o_s[...]
    gate = _silu(g_ref[...])
    nw = nw_ref[...]
    parts = []
    for h in range(2):
        oh = o[:, h * HG_HEAD:(h + 1) * HG_HEAD]
        inv = lax.rsqrt(jnp.mean(oh * oh, axis=-1, keepdims=True) + RMS_EPS)
        parts.append(oh * inv)
    y_ref[...] = (jnp.concatenate(parts, axis=1) * nw * gate).astype(y_ref.dtype)


def _hgrn(proj, lb_logits, norm_w, *, layer, batch, seq, width, col0, tb=512):
    n = proj.shape[0]
    nt = seq // tb
    w2 = 2 * HG_HEAD
    npair = width // w2
    cb = col0 // w2
    sec = width // w2
    sel_np, mk_np = _hgrn_consts()
    sel = jnp.asarray(sel_np, BF16)
    masks = jnp.asarray(mk_np, F32)

    def main(s):
        return pl.BlockSpec((tb, w2), lambda b, t, h: (b * nt + t, cb + s * sec + h))

    blk = lambda: pltpu.VMEM((tb, w2), F32)
    return pl.pallas_call(
        functools.partial(_hgrn_kernel, layer=layer),
        grid=(batch, nt, npair),
        in_specs=[
            main(0), main(1), main(2), main(3),
            pl.BlockSpec((lb_logits.shape[0], w2), lambda b, t, h: (0, h)),
            pl.BlockSpec((1, w2), lambda b, t, h: (0, h)),
            pl.BlockSpec(sel.shape, lambda b, t, h: (0, 0)),
            pl.BlockSpec(masks.shape, lambda b, t, h: (0, 0, 0)),
        ],
        out_specs=pl.BlockSpec((tb, w2), lambda b, t, h: (b * nt + t, h)),
        out_shape=jax.ShapeDtypeStruct((n, width), BF16),
        scratch_shapes=[pltpu.VMEM((npair, 2, HG_HEAD, HG_HEAD), F32)] + [blk() for _ in range(5)],
        compiler_params=_cparams(3),
        name="hgrn",
    )(proj, proj, proj, proj, lb_logits, norm_w, sel, masks)


def _pad_cols(w, to):
    return jnp.pad(w, ((0, 0), (0, to - w.shape[1])))


def _pad_rows(w, to):
    return jnp.pad(w, ((0, to - w.shape[0]), (0, 0)))


def kernel(x, ffn1_norm, ffn1_w_gate, ffn1_w_up, ffn1_w_down, mix_norm, w_in,
           rwkv_mu, rwkv_w0, rwkv_w2, rwkv_a0, rwkv_a2, rwkv_g2, rwkv_k_k, rwkv_k_a,
           rwkv_r_k, rwkv_ln_w, rwkv_ln_b, hgrn_lb_logits, hgrn_norm, w_out,
           ffn2_norm, ffn2_w_gate, ffn2_w_up, ffn2_w_down, final_norm):
    batch, seq, d = x.shape
    depth = ffn1_norm.shape[0]
    c = rwkv_w0.shape[1]
    ch = hgrn_norm.shape[1]
    dl, al, gl = rwkv_w2.shape[1], rwkv_a2.shape[1], rwkv_g2.shape[1]
    assert dl <= LANES and al <= LANES and gl <= 2 * LANES
    lora_col = 3 * c

    xf = x.reshape(batch * seq, d)
    row = lambda v: v.reshape(1, -1).astype(F32)
    for l in range(depth):
        xf = _ffn(xf, row(ffn1_norm[l]), ffn1_w_gate[l].astype(BF16), ffn1_w_up[l].astype(BF16),
                  ffn1_w_down[l].astype(BF16), row(final_norm), final_norm=False)

        wi = w_in[l]
        o = 3 * c
        secs = [wi[:, :o], _pad_cols(wi[:, o:o + dl], LANES), _pad_cols(wi[:, o + dl:o + dl + al], LANES),
                _pad_cols(wi[:, o + dl + al:o + dl + al + gl], 2 * LANES), wi[:, o + dl + al + gl:]]
        w_cat = jnp.concatenate(secs, axis=1).astype(BF16)
        mu = rwkv_mu[l].reshape(1, -1)
        mu_cat = jnp.concatenate([mu[:, :o], _pad_cols(mu[:, o:o + dl], LANES),
                                  _pad_cols(mu[:, o + dl:o + dl + al], LANES),
                                  _pad_cols(mu[:, o + dl + al:o + dl + al + gl], 2 * LANES)], axis=1)
        proj = _inproj(xf, row(mix_norm[l]), w_cat)

        y_r = _rwkv(proj, mu_cat, row(rwkv_w0[l]), row(rwkv_a0[l]), row(rwkv_k_k[l]), row(rwkv_k_a[l]),
                    row(rwkv_r_k[l]), row(rwkv_ln_w[l]), row(rwkv_ln_b[l]),
                    _pad_rows(rwkv_w2[l], LANES).astype(BF16), _pad_rows(rwkv_a2[l], LANES).astype(BF16),
                    _pad_rows(rwkv_g2[l], 2 * LANES).astype(BF16),
                    batch=batch, seq=seq, width=c, lora_col=lora_col)
        y_h = _hgrn(proj, hgrn_lb_logits.astype(F32), row(hgrn_norm[l]), layer=l,
                    batch=batch, seq=seq, width=ch, col0=lora_col + 4 * LANES)
        wo = w_out[l].astype(BF16)
        xf = _outproj(xf, y_r, y_h, wo[:c], wo[c:])

        xf = _ffn(xf, row(ffn2_norm[l]), ffn2_w_gate[l].astype(BF16), ffn2_w_up[l].astype(BF16),
                  ffn2_w_down[l].astype(BF16), row(final_norm), final_norm=(l == depth - 1))
    return xf.reshape(batch, seq, d)
```

```python
import functools

import numpy as np
import jax
import jax.numpy as jnp
from jax import lax
from jax.experimental import pallas as pl
from jax.experimental.pallas import tpu as pltpu

F32 = jnp.float32
BF16 = jnp.bfloat16

RMS_EPS = 1e-6
GN_EPS = 64e-5
FFN_RES = 0.5
KK_EPS = 1e-12

LANES = 128
RW_HEAD = 64
HG_HEAD = 128
CHUNK = 64
STACK = 2 * CHUNK
VMEM_LIMIT = 56 * 1024 * 1024
FFN_VMEM_LIMIT = 60 * 1024 * 1024
FFN_ROW_PANEL = 64

HG_LEVELS = (32, 16, 8, 4, 2, 1)


def _dot(a, b):
    return jnp.dot(a, b, preferred_element_type=F32)


def _dot_nt(a, b):
    return lax.dot_general(a, b, (((1,), (1,)), ((), ())), preferred_element_type=F32)


def _dot_tn(a, b):
    return lax.dot_general(a, b, (((0,), (0,)), ((), ())), preferred_element_type=F32)


def _bdot(a, b):
    return _dot(a.astype(BF16), b.astype(BF16))


def _split2(x):
    hi = x.astype(BF16)
    lo = (x - hi.astype(F32)).astype(BF16)
    return hi, lo


def _sigmoid(x):
    return 1.0 / (1.0 + jnp.exp(-x))


def _silu(x):
    return x * _sigmoid(x)


def _cparams(n_axes, vmem_limit=VMEM_LIMIT):
    return pltpu.CompilerParams(dimension_semantics=("arbitrary",) * n_axes,
                                vmem_limit_bytes=vmem_limit)


def _ffn_kernel(x_ref, g_ref, wg_ref, wu_ref, wd_ref, fn_ref, o_ref, h_ref, *, final_norm):
    j = pl.program_id(1)
    n_panels = x_ref.shape[0] // FFN_ROW_PANEL
    panel = lambda i: pl.ds(pl.multiple_of(i * FFN_ROW_PANEL, FFN_ROW_PANEL), FFN_ROW_PANEL)

    @pl.when(j == 0)
    def _():
        def norm_panel(i, carry):
            rows = panel(i)
            x = x_ref[rows, :]
            inv = lax.rsqrt(jnp.mean(x * x, axis=-1, keepdims=True) + RMS_EPS)
            h_ref[rows, :] = ((x * inv) * g_ref[...]).astype(BF16)
            return carry

        lax.fori_loop(0, n_panels, norm_panel, 0)
        o_ref[...] = jnp.zeros_like(o_ref)

    h = h_ref[...]
    gate = _dot(h, wg_ref[...])
    up = _dot(h, wu_ref[...])
    act = (_silu(gate) * up).astype(BF16)
    o_ref[...] += _dot(act, wd_ref[...])

    @pl.when(j == pl.num_programs(1) - 1)
    def _():
        def residual_panel(i, carry):
            rows = panel(i)
            y = x_ref[rows, :] + FFN_RES * o_ref[rows, :]
            if final_norm:
                inv = lax.rsqrt(jnp.mean(y * y, axis=-1, keepdims=True) + RMS_EPS)
                y = (y * inv) * fn_ref[...]
            o_ref[rows, :] = y
            return carry

        lax.fori_loop(0, n_panels, residual_panel, 0)


def _ffn(x, gain, wg, wu, wd, fnorm, *, final_norm, tm=1024, tf=512):
    n, d = x.shape
    f = wg.shape[1]
    assert n % tm == 0 and f % tf == 0
    return pl.pallas_call(
        functools.partial(_ffn_kernel, final_norm=final_norm),
        grid=(n // tm, f // tf),
        in_specs=[
            pl.BlockSpec((tm, d), lambda i, j: (i, 0)),
            pl.BlockSpec((1, d), lambda i, j: (0, 0)),
            pl.BlockSpec((d, tf), lambda i, j: (0, j)),
            pl.BlockSpec((d, tf), lambda i, j: (0, j)),
            pl.BlockSpec((tf, d), lambda i, j: (j, 0)),
            pl.BlockSpec((1, d), lambda i, j: (0, 0)),
        ],
        out_specs=pl.BlockSpec((tm, d), lambda i, j: (i, 0)),
        out_shape=jax.ShapeDtypeStruct((n, d), F32),
        scratch_shapes=[pltpu.VMEM((tm, d), BF16)],
        compiler_params=_cparams(2, FFN_VMEM_LIMIT),
        name="ffn_final" if final_norm else "ffn",
    )(x, gain, wg, wu, wd, fnorm)


def _inproj_kernel(x_ref, g_ref, w_ref, o_ref, h_ref):
    @pl.when(pl.program_id(1) == 0)
    def _():
        x = x_ref[...]
        inv = lax.rsqrt(jnp.mean(x * x, axis=-1, keepdims=True) + RMS_EPS)
        h_ref[...] = ((x * inv) * g_ref[...]).astype(BF16)

    o_ref[...] = _dot(h_ref[...], w_ref[...])


def _inproj(x, gain, w, *, tm=1024, tn=1536):
    n, d = x.shape
    c = w.shape[1]
    assert n % tm == 0 and c % tn == 0
    return pl.pallas_call(
        _inproj_kernel,
        grid=(n // tm, c // tn),
        in_specs=[
            pl.BlockSpec((tm, d), lambda i, j: (i, 0)),
            pl.BlockSpec((1, d), lambda i, j: (0, 0)),
            pl.BlockSpec((d, tn), lambda i, j: (0, j)),
        ],
        out_specs=pl.BlockSpec((tm, tn), lambda i, j: (i, j)),
        out_shape=jax.ShapeDtypeStruct((n, c), F32),
        scratch_shapes=[pltpu.VMEM((tm, d), BF16)],
        compiler_params=_cparams(2),
        name="inproj",
    )(x, gain, w)


def _outproj_kernel(x_ref, yr_ref, yh_ref, wr_ref, wh_ref, o_ref):
    o_ref[...] = x_ref[...] + _dot(yr_ref[...], wr_ref[...]) + _dot(yh_ref[...], wh_ref[...])


def _outproj(x, yr, yh, wr, wh, *, tm=512):
    n, d = x.shape
    cr, ch = yr.shape[1], yh.shape[1]
    assert n % tm == 0
    return pl.pallas_call(
        _outproj_kernel,
        grid=(n // tm,),
        in_specs=[
            pl.BlockSpec((tm, d), lambda i: (i, 0)),
            pl.BlockSpec((tm, cr), lambda i: (i, 0)),
            pl.BlockSpec((tm, ch), lambda i: (i, 0)),
            pl.BlockSpec((cr, d), lambda i: (0, 0)),
            pl.BlockSpec((ch, d), lambda i: (0, 0)),
        ],
        out_specs=pl.BlockSpec((tm, d), lambda i: (i, 0)),
        out_shape=jax.ShapeDtypeStruct((n, d), F32),
        compiler_params=_cparams(1),
        name="outproj",
    )(x, yr, yh, wr, wh)


def _rwkv_masks():
    i = np.arange(STACK)[:, None]
    j = np.arange(STACK)[None, :]
    same = (i // CHUNK) == (j // CHUNK)
    sl = same & (i > j)
    il = same & (i >= j)
    blk = lambda s: (i // s) == (j // s)
    d8 = sl & blk(8)
    c16 = sl & blk(16) & ~blk(8)
    c32 = sl & blk(32) & ~blk(16)
    c64 = sl & ~blk(32)
    return np.stack([sl, il, d8, c16, c32, c64]).astype(np.float32)


def _head_sum_matrix():
    i = np.arange(LANES)[:, None]
    j = np.arange(LANES)[None, :]
    return ((i // RW_HEAD) == (j // RW_HEAD)).astype(np.float32)


def _tri64():
    i = np.arange(CHUNK)[:, None]
    j = np.arange(CHUNK)[None, :]
    return (i >= j).astype(np.float32)


def _hgrn_consts():
    i = np.arange(STACK)[:, None]
    j = np.arange(STACK)[None, :]
    same = (i // CHUNK) == (j // CHUNK)
    cum = (same & (i >= j)).astype(np.float32)
    sels = [cum]
    masks = []
    for h in HG_LEVELS:
        m = (i // (2 * h)) * (2 * h) + h - 1
        ref = (same & (j <= m)).astype(np.float32)
        sign = np.where((i % (2 * h)) >= h, 1.0, -1.0).astype(np.float32)
        sels.append(sign * (cum - ref))
        masks.append(((i // (2 * h)) == (j // (2 * h))) & ((i % (2 * h)) >= h) & ((j % (2 * h)) < h))
    masks.append(i == j)
    return (np.concatenate(sels, axis=0), np.stack(masks).astype(np.float32))


def _shift(x, prev8, mu, first):
    rows = lax.broadcasted_iota(jnp.int32, x.shape, 0)
    prev_row = jnp.where(first, 0.0, prev8[7:8, :])
    xp = jnp.where(rows == 0, prev_row, pltpu.roll(x, shift=1, axis=0))
    return x + (xp - x) * mu


def _stack2(x):
    lane = lax.broadcasted_iota(jnp.int32, x.shape, 1)
    lo = lane < RW_HEAD
    return jnp.concatenate([jnp.where(lo, x, 0.0), jnp.where(lo, 0.0, x)], axis=0)


def _unit_lower_inverse(ms, mk_ref):
    shape = ms[0].shape
    eye = (lax.broadcasted_iota(jnp.int32, shape, 0) == lax.broadcasted_iota(jnp.int32, shape, 1)).astype(F32)
    both = lambda f, xs, ys: [f(x, y) for x, y in zip(xs, ys)]
    m8 = [m * mk_ref[2] for m in ms]
    m8_2 = both(_bdot, m8, m8)
    ts = [eye + m for m in m8]
    ts = [t + d for t, d in zip(ts, both(_bdot, ts, m8_2))]
    m8_4 = both(_bdot, m8_2, m8_2)
    ts = [t + d for t, d in zip(ts, both(_bdot, ts, m8_4))]
    for lvl in (3, 4, 5):
        tc = both(_bdot, ts, [m * mk_ref[lvl] for m in ms])
        ts = [t + d for t, d in zip(ts, both(_bdot, tc, ts))]
    return ts


def _rwkv_kernel(r_ref, k_ref, v_ref, lo_ref, rp_ref, kp_ref, vp_ref, lop_ref,
                 mur_ref, muk_ref, muv_ref, mul_ref,
                 w0_ref, a0_ref, kk_ref, ka_ref, rk_ref, lnw_ref, lnb_ref,
                 w2_ref, a2_ref, g2_ref, tri_ref, e2_ref, mk_ref,
                 y_ref,
                 s_ref, *scratch):
    t = pl.program_id(1)
    p = pl.program_id(2)
    first = t == 0
    tb = r_ref.shape[0]
    nch = tb // CHUNK
    ng = r_ref.shape[1] // LANES
    nhalf = len(scratch) // 7
    gh = ng // nhalf
    hw = gh * LANES
    part = lambda ref, h: ref[:, h * hw:(h + 1) * hw]
    lanes = [slice(g * LANES, (g + 1) * LANES) for g in range(gh)]

    @pl.when(first)
    def _():
        for g in range(ng):
            s_ref[p * ng + g] = jnp.zeros((LANES, LANES), F32)

    e2 = e2_ref[...]
    tri = tri_ref[...]
    head_sum = lambda x: jnp.concatenate([_bdot(x[:, ln], e2) for ln in lanes], axis=1)
    lora = _shift(lo_ref[...], lop_ref[...], mul_ref[...], first)
    txw = jnp.tanh(lora[:, 0:LANES]).astype(BF16)
    xa = lora[:, LANES:2 * LANES].astype(BF16)
    sxg = _sigmoid(lora[:, 2 * LANES:4 * LANES]).astype(BF16)

    def prologue(h):
        r_s, k_s, v_s, al_s, be_s, lw_s, _ = scratch[7 * h:7 * h + 7]
        r = _shift(part(r_ref, h), part(rp_ref, h), part(mur_ref, h), first)
        k = _shift(part(k_ref, h), part(kp_ref, h), part(muk_ref, h), first)
        v = _shift(part(v_ref, h), part(vp_ref, h), part(muv_ref, h), first)
        z = part(w0_ref, h) + _dot(txw, part(w2_ref, h))
        w = -(jnp.maximum(-z, 0.0) + jnp.log(1.0 + jnp.exp(-jnp.abs(z)))) - 0.5
        a = _sigmoid(part(a0_ref, h) + _dot(xa, part(a2_ref, h)))
        gate = _dot(sxg, part(g2_ref, h))
        kk = k * part(kk_ref, h)
        nrm = jnp.sqrt(head_sum(kk * kk))
        kk = kk / jnp.maximum(nrm, KK_EPS)
        kmod = k * (1.0 + (a - 1.0) * part(ka_ref, h))
        bonus = head_sum(r * kmod * part(rk_ref, h)) * v
        r_s[...] = r
        k_s[...] = kmod
        v_s[...] = v
        al_s[...] = -kk
        be_s[...] = kk * a
        lw_s[...] = -jnp.exp(w)
        return bonus, gate

    def stages(h):
        r_s, k_s, v_s, al_s, be_s, lw_s, _ = scratch[7 * h:7 * h + 7]
        return _rwkv_stages(r_s, k_s, v_s, al_s, be_s, lw_s, tri, mk_ref, nch, lanes)

    pro = [prologue(h) for h in range(nhalf)]
    st = [stages(h) for h in range(nhalf)]

    ss = [s_ref[p * ng + g] for g in range(ng)]
    for c in range(nch):
        for h in range(nhalf):
            y_s = scratch[7 * h + 6]
            tiles, glast, y_loc, q, a_c, b_c = st[h]
            for gl in range(gh):
                i = c * gh + gl
                rw, ln = tiles[i]
                s = ss[h * gh + gl]
                y = y_loc[i] + _dot_nt(q[i].astype(BF16), s.astype(BF16))
                y_s[rw, ln] = y[0:CHUNK] + y[CHUNK:STACK]
                ss[h * gh + gl] = s * jnp.exp(glast[i]) + _bdot(s, a_c[i]) + b_c[i]
    for g in range(ng):
        s_ref[p * ng + g] = ss[g]

    inv_n = 1.0 / RW_HEAD
    for h in range(nhalf):
        bonus, gate = pro[h]
        y = scratch[7 * h + 6][...]
        mean = head_sum(y) * inv_n
        d = y - mean
        var = head_sum(d * d) * inv_n
        yn = d * lax.rsqrt(var + GN_EPS) * part(lnw_ref, h) + part(lnb_ref, h)
        y_ref[:, h * hw:(h + 1) * hw] = ((yn + bonus) * gate).astype(y_ref.dtype)


def _rwkv_stages(r_s, k_s, v_s, al_s, be_s, lw_s, tri, mk_ref, nch, lanes):
    bf = lambda xs: [x.astype(BF16) for x in xs]
    cat0 = lambda xs, ys: [jnp.concatenate([x, y], axis=0) for x, y in zip(xs, ys)]
    cat1 = lambda xs, ys: [jnp.concatenate([x, y], axis=1) for x, y in zip(xs, ys)]
    tiles = [(pl.ds(c * CHUNK, CHUNK), ln) for c in range(nch) for ln in lanes]
    lwc = [lw_s[rw, ln] for rw, ln in tiles]
    hi, lo = _split2(jnp.concatenate(lwc, axis=1))
    gall = _dot(tri, jnp.concatenate([hi, lo], axis=0))
    gc = [gall[:, i * LANES:(i + 1) * LANES] for i in range(len(tiles))]
    glast = [g_[CHUNK - 1:CHUNK, :] for g_ in gc]
    e_in = [jnp.exp(g_) for g_ in gc]
    e_ex = [jnp.exp(g_ - l_) for g_, l_ in zip(gc, lwc)]
    e_inv = [jnp.exp(-g_) for g_ in gc]
    e_end = [jnp.exp(gl - g_) for gl, g_ in zip(glast, gc)]
    at = [_stack2(al_s[rw, ln] * e) for (rw, ln), e in zip(tiles, e_ex)]
    rt = [_stack2(r_s[rw, ln] * e) for (rw, ln), e in zip(tiles, e_in)]
    bt = [_stack2(be_s[rw, ln] * e) for (rw, ln), e in zip(tiles, e_inv)]
    kt = [_stack2(k_s[rw, ln] * e) for (rw, ln), e in zip(tiles, e_inv)]
    bh = [_stack2(be_s[rw, ln] * e) for (rw, ln), e in zip(tiles, e_end)]
    kh = [_stack2(k_s[rw, ln] * e) for (rw, ln), e in zip(tiles, e_end)]
    vs = [_stack2(v_s[rw, ln]) for rw, ln in tiles]

    g1 = [_dot_nt(x, y) for x, y in zip(bf(cat0(at, rt)), bf(cat0(bt, kt)))]
    m_ab = [g_[0:STACK, 0:STACK] * mk_ref[0] for g_ in g1]
    m_ak = [g_[0:STACK, STACK:2 * STACK] * mk_ref[0] for g_ in g1]
    m_rb = [g_[STACK:2 * STACK, 0:STACK] * mk_ref[1] for g_ in g1]
    m_rk = [g_[STACK:2 * STACK, STACK:2 * STACK] * mk_ref[1] for g_ in g1]

    tinv = _unit_lower_inverse(m_ab, mk_ref)
    x = [_bdot(m, v_) for m, v_ in zip(m_ak, vs)]
    tx = [_bdot(t_, z_) for t_, z_ in zip(tinv, cat1(x, at))]
    u_loc = [t_[:, 0:LANES] for t_ in tx]
    wm = [t_[:, LANES:2 * LANES] for t_ in tx]
    y_loc = [_bdot(m, z_) for m, z_ in zip(cat1(m_rb, m_rk), cat0(u_loc, vs))]
    q = [r_ + _bdot(m, w_) for r_, m, w_ in zip(rt, m_rb, wm)]

    a_c = [_dot_tn(w_, b_) for w_, b_ in zip(bf(wm), bf(bh))]
    b_c = [_dot_tn(z_, w_) for z_, w_ in zip(bf(cat0(u_loc, vs)), bf(cat0(bh, kh)))]
    return tiles, glast, y_loc, q, a_c, b_c


def _rwkv(proj, mu, w0, a0, k_k, k_a, r_k, ln_w, ln_b, w2p, a2p, g2p, *, batch, seq, width, lora_col,
          tb=512, pairs_per_step=4, parts=2):
    n = proj.shape[0]
    nt = seq // tb
    npair = width // LANES
    cw = pairs_per_step * LANES
    wb = width // cw
    lb4 = lora_col // (4 * LANES)
    rb8 = tb // 8
    assert npair % pairs_per_step == 0 and seq % tb == 0

    def main(sec):
        return pl.BlockSpec((tb, cw), lambda b, t, p: (b * nt + t, sec * wb + p))

    def prev(sec):
        return pl.BlockSpec((8, cw), lambda b, t, p: (jnp.maximum((b * nt + t) * rb8 - 1, 0), sec * wb + p))

    def vec(sec):
        return pl.BlockSpec((1, cw), lambda b, t, p: (0, sec * wb + p))

    pvec = pl.BlockSpec((1, cw), lambda b, t, p: (0, p))
    const2 = lambda shape: pl.BlockSpec(shape, lambda b, t, p: (0, 0))
    tri = jnp.asarray(np.concatenate([_tri64(), _tri64()], axis=1), BF16)
    e2 = jnp.asarray(_head_sum_matrix(), BF16)
    masks = jnp.asarray(_rwkv_masks(), F32)
    assert pairs_per_step % parts == 0
    blk = lambda: pltpu.VMEM((tb, cw // parts), F32)
    return pl.pallas_call(
        _rwkv_kernel,
        grid=(batch, nt, wb),
        in_specs=[
            main(0), main(1), main(2),
            pl.BlockSpec((tb, 4 * LANES), lambda b, t, p: (b * nt + t, lb4)),
            prev(0), prev(1), prev(2),
            pl.BlockSpec((8, 4 * LANES), lambda b, t, p: (jnp.maximum((b * nt + t) * rb8 - 1, 0), lb4)),
            vec(0), vec(1), vec(2),
            pl.BlockSpec((1, 4 * LANES), lambda b, t, p: (0, lb4)),
            pvec, pvec, pvec, pvec, pvec, pvec, pvec,
            pl.BlockSpec((LANES, cw), lambda b, t, p: (0, p)),
            pl.BlockSpec((LANES, cw), lambda b, t, p: (0, p)),
            pl.BlockSpec((2 * LANES, cw), lambda b, t, p: (0, p)),
            const2((CHUNK, 2 * CHUNK)), const2((LANES, LANES)),
            pl.BlockSpec((6, STACK, STACK), lambda b, t, p: (0, 0, 0)),
        ],
        out_specs=pl.BlockSpec((tb, cw), lambda b, t, p: (b * nt + t, p)),
        out_shape=jax.ShapeDtypeStruct((n, width), BF16),
        scratch_shapes=[pltpu.VMEM((npair, LANES, LANES), F32)] + [blk() for _ in range(7 * parts)],
        compiler_params=_cparams(3),
        name="rwkv",
    )(proj, proj, proj, proj, proj, proj, proj, proj, mu, mu, mu, mu,
      w0, a0, k_k, k_a, r_k, ln_w, ln_b, w2p, a2p, g2p, tri, e2, masks)


def _hgrn_kernel(q_ref, f_ref, i_ref, g_ref, lg_ref, nw_ref, sel_ref, mk_ref,
                 y_ref,
                 s_ref, q_s, k_s, v_s, lf_s, o_s, hl_s, *, layer):
    t = pl.program_id(1)
    hp = pl.program_id(2)
    tb = q_ref.shape[0]
    nch = tb // CHUNK
    nlev = len(HG_LEVELS)

    @pl.when(t == 0)
    def _():
        s_ref[hp] = jnp.zeros((2, HG_HEAD, HG_HEAD), F32)

    lg = lg_ref[...]
    mx = jnp.max(lg, axis=0, keepdims=True)
    ex = jnp.exp(lg - mx)
    lb = jnp.sum(ex[0:layer + 1, :], axis=0, keepdims=True) / jnp.sum(ex, axis=0, keepdims=True)
    forget = lb + (1.0 - lb) * _sigmoid(f_ref[...])
    q_s[...] = _silu(q_ref[...])
    k_s[...] = 1.0 - forget
    v_s[...] = i_ref[...]
    lf_s[...] = jnp.log(forget)

    def rows2(ref, rows):
        blk = ref[rows, :]
        return jnp.concatenate([blk[:, 0:HG_HEAD], blk[:, HG_HEAD:2 * HG_HEAD]], axis=0)

    cs = range(nch)
    rows = [pl.ds(c * CHUNK, CHUNK) for c in cs]
    qc = [rows2(q_s, rw) for rw in rows]
    kc = [rows2(k_s, rw) for rw in rows]
    vc = [rows2(v_s, rw) for rw in rows]
    hi, lo = _split2(jnp.concatenate([rows2(lf_s, rw) for rw in rows], axis=1))
    hl_s[...] = jnp.concatenate([hi, lo], axis=0)
    sel_all = _dot(sel_ref[...], hl_s[...])
    sel_dot = lambda blk: [sel_all[blk * STACK:(blk + 1) * STACK, c * HG_HEAD:(c + 1) * HG_HEAD] for c in cs]
    gc = sel_dot(0)
    qb = [q_.astype(BF16) for q_ in qc]
    kb = [k_.astype(BF16) for k_ in kc]
    a = [_dot_nt(q_, k_) * mk_ref[nlev] for q_, k_ in zip(qb, kb)]
    for li in range(nlev):
        ed = [jnp.exp(d_).astype(BF16) for d_ in sel_dot(li + 1)]
        a = [a_ + _dot_nt(q_ * e, k_ * e) * mk_ref[li] for a_, q_, k_, e in zip(a, qb, kb, ed)]
    o = [_bdot(a_, v_) for a_, v_ in zip(a, vc)]
    qg = [(q_ * jnp.exp(g_)).astype(BF16) for q_, g_ in zip(qc, gc)]
    s_pair = s_ref[hp]
    halves = []
    new = []
    for h in range(2):
        sl = slice(h * CHUNK, (h + 1) * CHUNK)
        glast = [g_[(h + 1) * CHUNK - 1:(h + 1) * CHUNK, :] for g_ in gc]
        khat = [(k_[sl] * jnp.exp(gl - g_[sl])).astype(BF16) for k_, gl, g_ in zip(kc, glast, gc)]
        delta = [_dot_tn(v_[sl].astype(BF16), k_) for v_, k_ in zip(vc, khat)]
        s = s_pair[h]
        s_in = []
        for c in cs:
            s_in.append(s.astype(BF16))
            s = s * jnp.exp(glast[c]) + delta[c]
        new.append(s)
        halves.append([o_[sl] + _dot_nt(q_[sl], s_) for o_, q_, s_ in zip(o, qg, s_in)])
    s_ref[hp] = jnp.stack(new)
    for c in cs:
        o_s[rows[c], :] = jnp.concatenate([halves[0][c], halves[1][c]], axis=1)

    o = o_s[...]
    gate = _silu(g_ref[...])
    nw = nw_ref[...]
    parts = []
    for h in range(2):
        oh = o[:, h * HG_HEAD:(h + 1) * HG_HEAD]
        inv = lax.rsqrt(jnp.mean(oh * oh, axis=-1, keepdims=True) + RMS_EPS)
        parts.append(oh * inv)
    y_ref[...] = (jnp.concatenate(parts, axis=1) * nw * gate).astype(y_ref.dtype)


def _hgrn(proj, lb_logits, norm_w, *, layer, batch, seq, width, col0, tb=512):
    n = proj.shape[0]
    nt = seq // tb
    w2 = 2 * HG_HEAD
    npair = width // w2
    cb = col0 // w2
    sec = width // w2
    sel_np, mk_np = _hgrn_consts()
    sel = jnp.asarray(np.concatenate([sel_np, sel_np], axis=1), BF16)
    masks = jnp.asarray(mk_np, F32)

    def main(s):
        return pl.BlockSpec((tb, w2), lambda b, t, h: (b * nt + t, cb + s * sec + h))

    blk = lambda: pltpu.VMEM((tb, w2), F32)
    return pl.pallas_call(
        functools.partial(_hgrn_kernel, layer=layer),
        grid=(batch, nt, npair),
        in_specs=[
            main(0), main(1), main(2), main(3),
            pl.BlockSpec((lb_logits.shape[0], w2), lambda b, t, h: (0, h)),
            pl.BlockSpec((1, w2), lambda b, t, h: (0, h)),
            pl.BlockSpec(sel.shape, lambda b, t, h: (0, 0)),
            pl.BlockSpec(masks.shape, lambda b, t, h: (0, 0, 0)),
        ],
        out_specs=pl.BlockSpec((tb, w2), lambda b, t, h: (b * nt + t, h)),
        out_shape=jax.ShapeDtypeStruct((n, width), BF16),
        scratch_shapes=[pltpu.VMEM((npair, 2, HG_HEAD, HG_HEAD), F32)] + [blk() for _ in range(5)]
        + [pltpu.VMEM((2 * STACK, (tb // CHUNK) * HG_HEAD), BF16)],
        compiler_params=_cparams(3),
        name="hgrn",
    )(proj, proj, proj, proj, lb_logits, norm_w, sel, masks)


def _pad_cols(w, to):
    return jnp.pad(w, ((0, 0), (0, to - w.shape[1])))


def _pad_rows(w, to):
    return jnp.pad(w, ((0, to - w.shape[0]), (0, 0)))


def kernel(x, ffn1_norm, ffn1_w_gate, ffn1_w_up, ffn1_w_down, mix_norm, w_in,
           rwkv_mu, rwkv_w0, rwkv_w2, rwkv_a0, rwkv_a2, rwkv_g2, rwkv_k_k, rwkv_k_a,
           rwkv_r_k, rwkv_ln_w, rwkv_ln_b, hgrn_lb_logits, hgrn_norm, w_out,
           ffn2_norm, ffn2_w_gate, ffn2_w_up, ffn2_w_down, final_norm):
    batch, seq, d = x.shape
    depth = ffn1_norm.shape[0]
    c = rwkv_w0.shape[1]
    ch = hgrn_norm.shape[1]
    dl, al, gl = rwkv_w2.shape[1], rwkv_a2.shape[1], rwkv_g2.shape[1]
    assert dl <= LANES and al <= LANES and gl <= 2 * LANES
    lora_col = 3 * c

    xf = x.reshape(batch * seq, d)
    row = lambda v: v.reshape(1, -1).astype(F32)
    for l in range(depth):
        xf = _ffn(xf, row(ffn1_norm[l]), ffn1_w_gate[l].astype(BF16), ffn1_w_up[l].astype(BF16),
                  ffn1_w_down[l].astype(BF16), row(final_norm), final_norm=False)

        wi = w_in[l]
        o = 3 * c
        secs = [wi[:, :o], _pad_cols(wi[:, o:o + dl], LANES), _pad_cols(wi[:, o + dl:o + dl + al], LANES),
                _pad_cols(wi[:, o + dl + al:o + dl + al + gl], 2 * LANES), wi[:, o + dl + al + gl:]]
        w_cat = jnp.concatenate(secs, axis=1).astype(BF16)
        mu = rwkv_mu[l].reshape(1, -1)
        mu_cat = jnp.concatenate([mu[:, :o], _pad_cols(mu[:, o:o + dl], LANES),
                                  _pad_cols(mu[:, o + dl:o + dl + al], LANES),
                                  _pad_cols(mu[:, o + dl + al:o + dl + al + gl], 2 * LANES)], axis=1)
        proj = _inproj(xf, row(mix_norm[l]), w_cat)

        y_r = _rwkv(proj, mu_cat, row(rwkv_w0[l]), row(rwkv_a0[l]), row(rwkv_k_k[l]), row(rwkv_k_a[l]),
                    row(rwkv_r_k[l]), row(rwkv_ln_w[l]), row(rwkv_ln_b[l]),
                    _pad_rows(rwkv_w2[l], LANES).astype(BF16), _pad_rows(rwkv_a2[l], LANES).astype(BF16),
                    _pad_rows(rwkv_g2[l], 2 * LANES).astype(BF16),
                    batch=batch, seq=seq, width=c, lora_col=lora_col)
        y_h = _hgrn(proj, hgrn_lb_logits.astype(F32), row(hgrn_norm[l]), layer=l,
                    batch=batch, seq=seq, width=ch, col0=lora_col + 4 * LANES)
        wo = w_out[l].astype(BF16)
        xf = _outproj(xf, y_r, y_h, wo[:c], wo[c:])

        xf = _ffn(xf, row(ffn2_norm[l]), ffn2_w_gate[l].astype(BF16), ffn2_w_up[l].astype(BF16),
                  ffn2_w_down[l].astype(BF16), row(final_norm), final_norm=(l == depth - 1))
    return xf.reshape(batch, seq, d)
```

```python
import functools

import numpy as np
import jax
import jax.numpy as jnp
from jax import lax
from jax.experimental import pallas as pl
from jax.experimental.pallas import tpu as pltpu

F32 = jnp.float32
BF16 = jnp.bfloat16

RMS_EPS = 1e-6
GN_EPS = 64e-5
FFN_RES = 0.5
KK_EPS = 1e-12

LANES = 128
RW_HEAD = 64
HG_HEAD = 128
CHUNK = 64
STACK = 2 * CHUNK
VMEM_LIMIT = 56 * 1024 * 1024
FFN_VMEM_LIMIT = 60 * 1024 * 1024
FFN_ROW_PANEL = 128

HG_LEVELS = (32, 16, 8, 4, 2, 1)


def _dot(a, b):
    return jnp.dot(a, b, preferred_element_type=F32)


def _dot_nt(a, b):
    return lax.dot_general(a, b, (((1,), (1,)), ((), ())), preferred_element_type=F32)


def _dot_tn(a, b):
    return lax.dot_general(a, b, (((0,), (0,)), ((), ())), preferred_element_type=F32)


def _bdot(a, b):
    return _dot(a.astype(BF16), b.astype(BF16))


def _split2(x):
    hi = x.astype(BF16)
    lo = (x - hi.astype(F32)).astype(BF16)
    return hi, lo


def _sigmoid(x):
    return 1.0 / (1.0 + jnp.exp(-x))


def _silu(x):
    return x * _sigmoid(x)


def _cparams(n_axes, vmem_limit=VMEM_LIMIT):
    return pltpu.CompilerParams(dimension_semantics=("arbitrary",) * n_axes,
                                vmem_limit_bytes=vmem_limit)


def _ffn_kernel(x_ref, g_ref, wg_ref, wu_ref, wd_ref, fn_ref, o_ref, h_ref, *, final_norm):
    j = pl.program_id(1)
    n_panels = x_ref.shape[0] // FFN_ROW_PANEL
    panel = lambda i: pl.ds(pl.multiple_of(i * FFN_ROW_PANEL, FFN_ROW_PANEL), FFN_ROW_PANEL)

    @pl.when(j == 0)
    def _():
        def norm_panel(i, carry):
            rows = panel(i)
            x = x_ref[rows, :]
            inv = lax.rsqrt(jnp.mean(x * x, axis=-1, keepdims=True) + RMS_EPS)
            h_ref[rows, :] = ((x * inv) * g_ref[...]).astype(BF16)
            return carry

        lax.fori_loop(0, n_panels, norm_panel, 0)
        o_ref[...] = jnp.zeros_like(o_ref)

    h = h_ref[...]
    gate = _dot(h, wg_ref[...])
    up = _dot(h, wu_ref[...])
    act = (_silu(gate) * up).astype(BF16)
    o_ref[...] += _dot(act, wd_ref[...])

    @pl.when(j == pl.num_programs(1) - 1)
    def _():
        def residual_panel(i, carry):
            rows = panel(i)
            y = x_ref[rows, :] + FFN_RES * o_ref[rows, :]
            if final_norm:
                inv = lax.rsqrt(jnp.mean(y * y, axis=-1, keepdims=True) + RMS_EPS)
                y = (y * inv) * fn_ref[...]
            o_ref[rows, :] = y
            return carry

        lax.fori_loop(0, n_panels, residual_panel, 0)


def _ffn(x, gain, wg, wu, wd, fnorm, *, final_norm, tm=1024, tf=512):
    n, d = x.shape
    f = wg.shape[1]
    assert n % tm == 0 and f % tf == 0
    return pl.pallas_call(
        functools.partial(_ffn_kernel, final_norm=final_norm),
        grid=(n // tm, f // tf),
        in_specs=[
            pl.BlockSpec((tm, d), lambda i, j: (i, 0)),
            pl.BlockSpec((1, d), lambda i, j: (0, 0)),
            pl.BlockSpec((d, tf), lambda i, j: (0, j)),
            pl.BlockSpec((d, tf), lambda i, j: (0, j)),
            pl.BlockSpec((tf, d), lambda i, j: (j, 0)),
            pl.BlockSpec((1, d), lambda i, j: (0, 0)),
        ],
        out_specs=pl.BlockSpec((tm, d), lambda i, j: (i, 0)),
        out_shape=jax.ShapeDtypeStruct((n, d), F32),
        scratch_shapes=[pltpu.VMEM((tm, d), BF16)],
        compiler_params=_cparams(2, FFN_VMEM_LIMIT),
        name="ffn_final" if final_norm else "ffn",
    )(x, gain, wg, wu, wd, fnorm)


def _inproj_kernel(x_ref, g_ref, w_ref, o_ref, h_ref):
    @pl.when(pl.program_id(1) == 0)
    def _():
        x = x_ref[...]
        inv = lax.rsqrt(jnp.mean(x * x, axis=-1, keepdims=True) + RMS_EPS)
        h_ref[...] = ((x * inv) * g_ref[...]).astype(BF16)

    o_ref[...] = _dot(h_ref[...], w_ref[...])


def _inproj(x, gain, w, *, tm=1024, tn=1536):
    n, d = x.shape
    c = w.shape[1]
    assert n % tm == 0 and c % tn == 0
    return pl.pallas_call(
        _inproj_kernel,
        grid=(n // tm, c // tn),
        in_specs=[
            pl.BlockSpec((tm, d), lambda i, j: (i, 0)),
            pl.BlockSpec((1, d), lambda i, j: (0, 0)),
            pl.BlockSpec((d, tn), lambda i, j: (0, j)),
        ],
        out_specs=pl.BlockSpec((tm, tn), lambda i, j: (i, j)),
        out_shape=jax.ShapeDtypeStruct((n, c), F32),
        scratch_shapes=[pltpu.VMEM((tm, d), BF16)],
        compiler_params=_cparams(2),
        name="inproj",
    )(x, gain, w)


def _outproj_kernel(x_ref, yr_ref, yh_ref, wr_ref, wh_ref, o_ref):
    o_ref[...] = x_ref[...] + _dot(yr_ref[...], wr_ref[...]) + _dot(yh_ref[...], wh_ref[...])


def _outproj(x, yr, yh, wr, wh, *, tm=512):
    n, d = x.shape
    cr, ch = yr.shape[1], yh.shape[1]
    assert n % tm == 0
    return pl.pallas_call(
        _outproj_kernel,
        grid=(n // tm,),
        in_specs=[
            pl.BlockSpec((tm, d), lambda i: (i, 0)),
            pl.BlockSpec((tm, cr), lambda i: (i, 0)),
            pl.BlockSpec((tm, ch), lambda i: (i, 0)),
            pl.BlockSpec((cr, d), lambda i: (0, 0)),
            pl.BlockSpec((ch, d), lambda i: (0, 0)),
        ],
        out_specs=pl.BlockSpec((tm, d), lambda i: (i, 0)),
        out_shape=jax.ShapeDtypeStruct((n, d), F32),
        compiler_params=_cparams(1),
        name="outproj",
    )(x, yr, yh, wr, wh)


def _rwkv_masks():
    i = np.arange(STACK)[:, None]
    j = np.arange(STACK)[None, :]
    same = (i // CHUNK) == (j // CHUNK)
    sl = same & (i > j)
    il = same & (i >= j)
    blk = lambda s: (i // s) == (j // s)
    d8 = sl & blk(8)
    c16 = sl & blk(16) & ~blk(8)
    c32 = sl & blk(32) & ~blk(16)
    c64 = sl & ~blk(32)
    return np.stack([sl, il, d8, c16, c32, c64]).astype(np.float32)


def _head_sum_matrix():
    i = np.arange(LANES)[:, None]
    j = np.arange(LANES)[None, :]
    return ((i // RW_HEAD) == (j // RW_HEAD)).astype(np.float32)


def _tri64():
    i = np.arange(CHUNK)[:, None]
    j = np.arange(CHUNK)[None, :]
    return (i >= j).astype(np.float32)


def _hgrn_consts():
    i = np.arange(STACK)[:, None]
    j = np.arange(STACK)[None, :]
    same = (i // CHUNK) == (j // CHUNK)
    cum = (same & (i >= j)).astype(np.float32)
    sels = [cum]
    masks = []
    for h in HG_LEVELS:
        m = (i // (2 * h)) * (2 * h) + h - 1
        ref = (same & (j <= m)).astype(np.float32)
        sign = np.where((i % (2 * h)) >= h, 1.0, -1.0).astype(np.float32)
        sels.append(sign * (cum - ref))
        masks.append(((i // (2 * h)) == (j // (2 * h))) & ((i % (2 * h)) >= h) & ((j % (2 * h)) < h))
    masks.append(i == j)
    return (np.concatenate(sels, axis=0), np.stack(masks).astype(np.float32))


def _shift(x, prev8, mu, first):
    rows = lax.broadcasted_iota(jnp.int32, x.shape, 0)
    prev_row = jnp.where(first, 0.0, prev8[7:8, :])
    xp = jnp.where(rows == 0, prev_row, pltpu.roll(x, shift=1, axis=0))
    return x + (xp - x) * mu


def _stack2(x):
    lane = lax.broadcasted_iota(jnp.int32, x.shape, 1)
    lo = lane < RW_HEAD
    return jnp.concatenate([jnp.where(lo, x, 0.0), jnp.where(lo, 0.0, x)], axis=0)


def _unit_lower_inverse(ms, mk_ref):
    shape = ms[0].shape
    eye = (lax.broadcasted_iota(jnp.int32, shape, 0) == lax.broadcasted_iota(jnp.int32, shape, 1)).astype(F32)
    both = lambda f, xs, ys: [f(x, y) for x, y in zip(xs, ys)]
    m8 = [m * mk_ref[2] for m in ms]
    m8_2 = both(_bdot, m8, m8)
    ts = [eye + m for m in m8]
    ts = [t + d for t, d in zip(ts, both(_bdot, ts, m8_2))]
    m8_4 = both(_bdot, m8_2, m8_2)
    ts = [t + d for t, d in zip(ts, both(_bdot, ts, m8_4))]
    for lvl in (3, 4, 5):
        tc = both(_bdot, ts, [m * mk_ref[lvl] for m in ms])
        ts = [t + d for t, d in zip(ts, both(_bdot, tc, ts))]
    return ts


def _rwkv_kernel(r_ref, k_ref, v_ref, lo_ref, rp_ref, kp_ref, vp_ref, lop_ref,
                 mur_ref, muk_ref, muv_ref, mul_ref,
                 w0_ref, a0_ref, kk_ref, ka_ref, rk_ref, lnw_ref, lnb_ref,
                 w2_ref, a2_ref, g2_ref, tri_ref, e2_ref, mk_ref,
                 y_ref,
                 s_ref, *scratch):
    t = pl.program_id(1)
    p = pl.program_id(2)
    first = t == 0
    tb = r_ref.shape[0]
    nch = tb // CHUNK
    ng = r_ref.shape[1] // LANES
    nhalf = len(scratch) // 7
    gh = ng // nhalf
    hw = gh * LANES
    part = lambda ref, h: ref[:, h * hw:(h + 1) * hw]
    lanes = [slice(g * LANES, (g + 1) * LANES) for g in range(gh)]

    @pl.when(first)
    def _():
        for g in range(ng):
            s_ref[p * ng + g] = jnp.zeros((LANES, LANES), F32)

    e2 = e2_ref[...]
    tri = tri_ref[...]
    head_sum = lambda x: jnp.concatenate([_bdot(x[:, ln], e2) for ln in lanes], axis=1)
    lora = _shift(lo_ref[...], lop_ref[...], mul_ref[...], first)
    txw = jnp.tanh(lora[:, 0:LANES]).astype(BF16)
    xa = lora[:, LANES:2 * LANES].astype(BF16)
    sxg = _sigmoid(lora[:, 2 * LANES:4 * LANES]).astype(BF16)

    def prologue(h):
        r_s, k_s, v_s, al_s, be_s, lw_s, _ = scratch[7 * h:7 * h + 7]
        r = _shift(part(r_ref, h), part(rp_ref, h), part(mur_ref, h), first)
        k = _shift(part(k_ref, h), part(kp_ref, h), part(muk_ref, h), first)
        v = _shift(part(v_ref, h), part(vp_ref, h), part(muv_ref, h), first)
        z = part(w0_ref, h) + _dot(txw, part(w2_ref, h))
        w = -(jnp.maximum(-z, 0.0) + jnp.log(1.0 + jnp.exp(-jnp.abs(z)))) - 0.5
        a = _sigmoid(part(a0_ref, h) + _dot(xa, part(a2_ref, h)))
        gate = _dot(sxg, part(g2_ref, h))
        kk = k * part(kk_ref, h)
        nrm = jnp.sqrt(head_sum(kk * kk))
        kk = kk / jnp.maximum(nrm, KK_EPS)
        kmod = k * (1.0 + (a - 1.0) * part(ka_ref, h))
        bonus = head_sum(r * kmod * part(rk_ref, h)) * v
        r_s[...] = r
        k_s[...] = kmod
        v_s[...] = v
        al_s[...] = -kk
        be_s[...] = kk * a
        lw_s[...] = -jnp.exp(w)
        return bonus, gate

    def stages(h):
        r_s, k_s, v_s, al_s, be_s, lw_s, _ = scratch[7 * h:7 * h + 7]
        return _rwkv_stages(r_s, k_s, v_s, al_s, be_s, lw_s, tri, mk_ref, nch, lanes)

    pro = [prologue(h) for h in range(nhalf)]
    st = [stages(h) for h in range(nhalf)]

    ss = [s_ref[p * ng + g] for g in range(ng)]
    for c in range(nch):
        for h in range(nhalf):
            y_s = scratch[7 * h + 6]
            tiles, glast, y_loc, q, a_c, b_c = st[h]
            for gl in range(gh):
                i = c * gh + gl
                rw, ln = tiles[i]
                s = ss[h * gh + gl]
                y = y_loc[i] + _dot_nt(q[i].astype(BF16), s.astype(BF16))
                y_s[rw, ln] = y[0:CHUNK] + y[CHUNK:STACK]
                ss[h * gh + gl] = s * jnp.exp(glast[i]) + _bdot(s, a_c[i]) + b_c[i]
    for g in range(ng):
        s_ref[p * ng + g] = ss[g]

    inv_n = 1.0 / RW_HEAD
    for h in range(nhalf):
        bonus, gate = pro[h]
        y = scratch[7 * h + 6][...]
        mean = head_sum(y) * inv_n
        d = y - mean
        var = head_sum(d * d) * inv_n
        yn = d * lax.rsqrt(var + GN_EPS) * part(lnw_ref, h) + part(lnb_ref, h)
        y_ref[:, h * hw:(h + 1) * hw] = ((yn + bonus) * gate).astype(y_ref.dtype)


def _rwkv_stages(r_s, k_s, v_s, al_s, be_s, lw_s, tri, mk_ref, nch, lanes):
    bf = lambda xs: [x.astype(BF16) for x in xs]
    cat0 = lambda xs, ys: [jnp.concatenate([x, y], axis=0) for x, y in zip(xs, ys)]
    cat1 = lambda xs, ys: [jnp.concatenate([x, y], axis=1) for x, y in zip(xs, ys)]
    tiles = [(pl.ds(c * CHUNK, CHUNK), ln) for c in range(nch) for ln in lanes]
    lwc = [lw_s[rw, ln] for rw, ln in tiles]
    hi, lo = _split2(jnp.concatenate(lwc, axis=1))
    gall = _dot(tri, jnp.concatenate([hi, lo], axis=0))
    gc = [gall[:, i * LANES:(i + 1) * LANES] for i in range(len(tiles))]
    glast = [g_[CHUNK - 1:CHUNK, :] for g_ in gc]
    e_in = [jnp.exp(g_) for g_ in gc]
    e_ex = [jnp.exp(g_ - l_) for g_, l_ in zip(gc, lwc)]
    e_inv = [jnp.exp(-g_) for g_ in gc]
    e_end = [jnp.exp(gl - g_) for gl, g_ in zip(glast, gc)]
    at = [_stack2(al_s[rw, ln] * e) for (rw, ln), e in zip(tiles, e_ex)]
    rt = [_stack2(r_s[rw, ln] * e) for (rw, ln), e in zip(tiles, e_in)]
    bt = [_stack2(be_s[rw, ln] * e) for (rw, ln), e in zip(tiles, e_inv)]
    kt = [_stack2(k_s[rw, ln] * e) for (rw, ln), e in zip(tiles, e_inv)]
    bh = [_stack2(be_s[rw, ln] * e) for (rw, ln), e in zip(tiles, e_end)]
    kh = [_stack2(k_s[rw, ln] * e) for (rw, ln), e in zip(tiles, e_end)]
    vs = [_stack2(v_s[rw, ln]) for rw, ln in tiles]

    g1 = [_dot_nt(x, y) for x, y in zip(bf(cat0(at, rt)), bf(cat0(bt, kt)))]
    m_ab = [g_[0:STACK, 0:STACK] * mk_ref[0] for g_ in g1]
    m_ak = [g_[0:STACK, STACK:2 * STACK] * mk_ref[0] for g_ in g1]
    m_rb = [g_[STACK:2 * STACK, 0:STACK] * mk_ref[1] for g_ in g1]
    m_rk = [g_[STACK:2 * STACK, STACK:2 * STACK] * mk_ref[1] for g_ in g1]

    tinv = _unit_lower_inverse(m_ab, mk_ref)
    x = [_bdot(m, v_) for m, v_ in zip(m_ak, vs)]
    tx = [_bdot(t_, z_) for t_, z_ in zip(tinv, cat1(x, at))]
    u_loc = [t_[:, 0:LANES] for t_ in tx]
    wm = [t_[:, LANES:2 * LANES] for t_ in tx]
    y_loc = [_bdot(m, z_) for m, z_ in zip(cat1(m_rb, m_rk), cat0(u_loc, vs))]
    q = [r_ + _bdot(m, w_) for r_, m, w_ in zip(rt, m_rb, wm)]

    a_c = [_dot_tn(w_, b_) for w_, b_ in zip(bf(wm), bf(bh))]
    b_c = [_dot_tn(z_, w_) for z_, w_ in zip(bf(cat0(u_loc, vs)), bf(cat0(bh, kh)))]
    return tiles, glast, y_loc, q, a_c, b_c


def _rwkv(proj, mu, w0, a0, k_k, k_a, r_k, ln_w, ln_b, w2p, a2p, g2p, *, batch, seq, width, lora_col,
          tb=512, pairs_per_step=4, parts=2):
    n = proj.shape[0]
    nt = seq // tb
    npair = width // LANES
    cw = pairs_per_step * LANES
    wb = width // cw
    lb4 = lora_col // (4 * LANES)
    rb8 = tb // 8
    assert npair % pairs_per_step == 0 and seq % tb == 0

    def main(sec):
        return pl.BlockSpec((tb, cw), lambda b, t, p: (b * nt + t, sec * wb + p))

    def prev(sec):
        return pl.BlockSpec((8, cw), lambda b, t, p: (jnp.maximum((b * nt + t) * rb8 - 1, 0), sec * wb + p))

    def vec(sec):
        return pl.BlockSpec((1, cw), lambda b, t, p: (0, sec * wb + p))

    pvec = pl.BlockSpec((1, cw), lambda b, t, p: (0, p))
    const2 = lambda shape: pl.BlockSpec(shape, lambda b, t, p: (0, 0))
    tri = jnp.asarray(np.concatenate([_tri64(), _tri64()], axis=1), BF16)
    e2 = jnp.asarray(_head_sum_matrix(), BF16)
    masks = jnp.asarray(_rwkv_masks(), F32)
    assert pairs_per_step % parts == 0
    blk = lambda: pltpu.VMEM((tb, cw // parts), F32)
    return pl.pallas_call(
        _rwkv_kernel,
        grid=(batch, nt, wb),
        in_specs=[
            main(0), main(1), main(2),
            pl.BlockSpec((tb, 4 * LANES), lambda b, t, p: (b * nt + t, lb4)),
            prev(0), prev(1), prev(2),
            pl.BlockSpec((8, 4 * LANES), lambda b, t, p: (jnp.maximum((b * nt + t) * rb8 - 1, 0), lb4)),
            vec(0), vec(1), vec(2),
            pl.BlockSpec((1, 4 * LANES), lambda b, t, p: (0, lb4)),
            pvec, pvec, pvec, pvec, pvec, pvec, pvec,
            pl.BlockSpec((LANES, cw), lambda b, t, p: (0, p)),
            pl.BlockSpec((LANES, cw), lambda b, t, p: (0, p)),
            pl.BlockSpec((2 * LANES, cw), lambda b, t, p: (0, p)),
            const2((CHUNK, 2 * CHUNK)), const2((LANES, LANES)),
            pl.BlockSpec((6, STACK, STACK), lambda b, t, p: (0, 0, 0)),
        ],
        out_specs=pl.BlockSpec((tb, cw), lambda b, t, p: (b * nt + t, p)),
        out_shape=jax.ShapeDtypeStruct((n, width), BF16),
        scratch_shapes=[pltpu.VMEM((npair, LANES, LANES), F32)] + [blk() for _ in range(7 * parts)],
        compiler_params=_cparams(3),
        name="rwkv",
    )(proj, proj, proj, proj, proj, proj, proj, proj, mu, mu, mu, mu,
      w0, a0, k_k, k_a, r_k, ln_w, ln_b, w2p, a2p, g2p, tri, e2, masks)


def _hgrn_kernel(q_ref, f_ref, i_ref, g_ref, lg_ref, nw_ref, sel_ref, mk_ref,
                 y_ref,
                 s_ref, q_s, k_s, v_s, lf_s, o_s, hl_s, *, layer):
    t = pl.program_id(1)
    hp = pl.program_id(2)
    tb = q_ref.shape[0]
    nch = tb // CHUNK
    nlev = len(HG_LEVELS)
    w2 = 2 * HG_HEAD
    npr = q_ref.shape[1] // w2

    @pl.when(t == 0)
    def _():
        for pr in range(npr):
            s_ref[hp * npr + pr] = jnp.zeros((2, HG_HEAD, HG_HEAD), F32)

    lg = lg_ref[...]
    mx = jnp.max(lg, axis=0, keepdims=True)
    ex = jnp.exp(lg - mx)
    lb = jnp.sum(ex[0:layer + 1, :], axis=0, keepdims=True) / jnp.sum(ex, axis=0, keepdims=True)
    forget = lb + (1.0 - lb) * _sigmoid(f_ref[...])
    q_s[...] = _silu(q_ref[...])
    k_s[...] = 1.0 - forget
    v_s[...] = i_ref[...]
    lf_s[...] = jnp.log(forget)

    def rows2(ref, tile):
        rw, pr = tile
        blk = ref[rw, pr * w2:(pr + 1) * w2]
        return jnp.concatenate([blk[:, 0:HG_HEAD], blk[:, HG_HEAD:w2]], axis=0)

    tiles = [(pl.ds(c * CHUNK, CHUNK), pr) for c in range(nch) for pr in range(npr)]
    cs = range(len(tiles))
    qc = [rows2(q_s, tl) for tl in tiles]
    kc = [rows2(k_s, tl) for tl in tiles]
    vc = [rows2(v_s, tl) for tl in tiles]
    hi, lo = _split2(jnp.concatenate([rows2(lf_s, tl) for tl in tiles], axis=1))
    hl_s[...] = jnp.concatenate([hi, lo], axis=0)
    sel_all = _dot(sel_ref[...], hl_s[...])
    sel_dot = lambda blk: [sel_all[blk * STACK:(blk + 1) * STACK, c * HG_HEAD:(c + 1) * HG_HEAD] for c in cs]
    gc = sel_dot(0)
    qb = [q_.astype(BF16) for q_ in qc]
    kb = [k_.astype(BF16) for k_ in kc]
    a = [_dot_nt(q_, k_) * mk_ref[nlev] for q_, k_ in zip(qb, kb)]
    for li in range(nlev):
        ed = [jnp.exp(d_).astype(BF16) for d_ in sel_dot(li + 1)]
        a = [a_ + _dot_nt(q_ * e, k_ * e) * mk_ref[li] for a_, q_, k_, e in zip(a, qb, kb, ed)]
    o = [_bdot(a_, v_) for a_, v_ in zip(a, vc)]
    qg = [(q_ * jnp.exp(g_)).astype(BF16) for q_, g_ in zip(qc, gc)]
    halves = []
    new = [[None, None] for _ in range(npr)]
    for h in range(2):
        sl = slice(h * CHUNK, (h + 1) * CHUNK)
        glast = [g_[(h + 1) * CHUNK - 1:(h + 1) * CHUNK, :] for g_ in gc]
        khat = [(k_[sl] * jnp.exp(gl - g_[sl])).astype(BF16) for k_, gl, g_ in zip(kc, glast, gc)]
        delta = [_dot_tn(v_[sl].astype(BF16), k_) for v_, k_ in zip(vc, khat)]
        ss = [s_ref[hp * npr + pr][h] for pr in range(npr)]
        s_in = []
        for c in cs:
            pr = tiles[c][1]
            s_in.append(ss[pr].astype(BF16))
            ss[pr] = ss[pr] * jnp.exp(glast[c]) + delta[c]
        for pr in range(npr):
            new[pr][h] = ss[pr]
        halves.append([o_[sl] + _dot_nt(q_[sl], s_) for o_, q_, s_ in zip(o, qg, s_in)])
    for pr in range(npr):
        s_ref[hp * npr + pr] = jnp.stack(new[pr])
    for c, (rw, pr) in enumerate(tiles):
        o_s[rw, pr * w2:(pr + 1) * w2] = jnp.concatenate([halves[0][c], halves[1][c]], axis=1)

    o = o_s[...]
    gate = _silu(g_ref[...])
    nw = nw_ref[...]
    parts = []
    for h in range(2 * npr):
        oh = o[:, h * HG_HEAD:(h + 1) * HG_HEAD]
        inv = lax.rsqrt(jnp.mean(oh * oh, axis=-1, keepdims=True) + RMS_EPS)
        parts.append(oh * inv)
    y_ref[...] = (jnp.concatenate(parts, axis=1) * nw * gate).astype(y_ref.dtype)


def _hgrn(proj, lb_logits, norm_w, *, layer, batch, seq, width, col0, tb=512, pairs_per_step=2):
    n = proj.shape[0]
    nt = seq // tb
    npair = width // (2 * HG_HEAD)
    cw = pairs_per_step * 2 * HG_HEAD
    cb = col0 // cw
    sec = width // cw
    assert npair % pairs_per_step == 0 and col0 % cw == 0 and seq % tb == 0
    sel_np, mk_np = _hgrn_consts()
    sel = jnp.asarray(np.concatenate([sel_np, sel_np], axis=1), BF16)
    masks = jnp.asarray(mk_np, F32)

    def main(s):
        return pl.BlockSpec((tb, cw), lambda b, t, h: (b * nt + t, cb + s * sec + h))

    blk = lambda: pltpu.VMEM((tb, cw), F32)
    return pl.pallas_call(
        functools.partial(_hgrn_kernel, layer=layer),
        grid=(batch, nt, sec),
        in_specs=[
            main(0), main(1), main(2), main(3),
            pl.BlockSpec((lb_logits.shape[0], cw), lambda b, t, h: (0, h)),
            pl.BlockSpec((1, cw), lambda b, t, h: (0, h)),
            pl.BlockSpec(sel.shape, lambda b, t, h: (0, 0)),
            pl.BlockSpec(masks.shape, lambda b, t, h: (0, 0, 0)),
        ],
        out_specs=pl.BlockSpec((tb, cw), lambda b, t, h: (b * nt + t, h)),
        out_shape=jax.ShapeDtypeStruct((n, width), BF16),
        scratch_shapes=[pltpu.VMEM((npair, 2, HG_HEAD, HG_HEAD), F32)] + [blk() for _ in range(5)]
        + [pltpu.VMEM((2 * STACK, (tb // CHUNK) * pairs_per_step * HG_HEAD), BF16)],
        compiler_params=_cparams(3),
        name="hgrn",
    )(proj, proj, proj, proj, lb_logits, norm_w, sel, masks)


def _pad_cols(w, to):
    return jnp.pad(w, ((0, 0), (0, to - w.shape[1])))


def _pad_rows(w, to):
    return jnp.pad(w, ((0, to - w.shape[0]), (0, 0)))


def kernel(x, ffn1_norm, ffn1_w_gate, ffn1_w_up, ffn1_w_down, mix_norm, w_in,
           rwkv_mu, rwkv_w0, rwkv_w2, rwkv_a0, rwkv_a2, rwkv_g2, rwkv_k_k, rwkv_k_a,
           rwkv_r_k, rwkv_ln_w, rwkv_ln_b, hgrn_lb_logits, hgrn_norm, w_out,
           ffn2_norm, ffn2_w_gate, ffn2_w_up, ffn2_w_down, final_norm):
    batch, seq, d = x.shape
    depth = ffn1_norm.shape[0]
    c = rwkv_w0.shape[1]
    ch = hgrn_norm.shape[1]
    dl, al, gl = rwkv_w2.shape[1], rwkv_a2.shape[1], rwkv_g2.shape[1]
    assert dl <= LANES and al <= LANES and gl <= 2 * LANES
    lora_col = 3 * c

    xf = x.reshape(batch * seq, d)
    row = lambda v: v.reshape(1, -1).astype(F32)
    for l in range(depth):
        xf = _ffn(xf, row(ffn1_norm[l]), ffn1_w_gate[l].astype(BF16), ffn1_w_up[l].astype(BF16),
                  ffn1_w_down[l].astype(BF16), row(final_norm), final_norm=False)

        wi = w_in[l]
        o = 3 * c
        secs = [wi[:, :o], _pad_cols(wi[:, o:o + dl], LANES), _pad_cols(wi[:, o + dl:o + dl + al], LANES),
                _pad_cols(wi[:, o + dl + al:o + dl + al + gl], 2 * LANES), wi[:, o + dl + al + gl:]]
        w_cat = jnp.concatenate(secs, axis=1).astype(BF16)
        mu = rwkv_mu[l].reshape(1, -1)
        mu_cat = jnp.concatenate([mu[:, :o], _pad_cols(mu[:, o:o + dl], LANES),
                                  _pad_cols(mu[:, o + dl:o + dl + al], LANES),
                                  _pad_cols(mu[:, o + dl + al:o + dl + al + gl], 2 * LANES)], axis=1)
        proj = _inproj(xf, row(mix_norm[l]), w_cat)

        y_r = _rwkv(proj, mu_cat, row(rwkv_w0[l]), row(rwkv_a0[l]), row(rwkv_k_k[l]), row(rwkv_k_a[l]),
                    row(rwkv_r_k[l]), row(rwkv_ln_w[l]), row(rwkv_ln_b[l]),
                    _pad_rows(rwkv_w2[l], LANES).astype(BF16), _pad_rows(rwkv_a2[l], LANES).astype(BF16),
                    _pad_rows(rwkv_g2[l], 2 * LANES).astype(BF16),
                    batch=batch, seq=seq, width=c, lora_col=lora_col)
        y_h = _hgrn(proj, hgrn_lb_logits.astype(F32), row(hgrn_norm[l]), layer=l,
                    batch=batch, seq=seq, width=ch, col0=lora_col + 4 * LANES)
        wo = w_out[l].astype(BF16)
        xf = _outproj(xf, y_r, y_h, wo[:c], wo[c:])

        xf = _ffn(xf, row(ffn2_norm[l]), ffn2_w_gate[l].astype(BF16), ffn2_w_up[l].astype(BF16),
                  ffn2_w_down[l].astype(BF16), row(final_norm), final_norm=(l == depth - 1))
    return xf.reshape(batch, seq, d)
```

```python
import functools

import numpy as np
import jax
import jax.numpy as jnp
from jax import lax
from jax.experimental import pallas as pl
from jax.experimental.pallas import tpu as pltpu

F32 = jnp.float32
BF16 = jnp.bfloat16

RMS_EPS = 1e-6
GN_EPS = 64e-5
FFN_RES = 0.5
KK_EPS = 1e-12

LANES = 128
RW_HEAD = 64
HG_HEAD = 128
CHUNK = 64
STACK = 2 * CHUNK
VMEM_LIMIT = 56 * 1024 * 1024
FFN_VMEM_LIMIT = 60 * 1024 * 1024
FFN_ROW_PANEL = 128

HG_LEVELS = (32, 16, 8, 4, 2, 1)


def _dot(a, b):
    return jnp.dot(a, b, preferred_element_type=F32)


def _dot_nt(a, b):
    return lax.dot_general(a, b, (((1,), (1,)), ((), ())), preferred_element_type=F32)


def _dot_tn(a, b):
    return lax.dot_general(a, b, (((0,), (0,)), ((), ())), preferred_element_type=F32)


def _bdot(a, b):
    return _dot(a.astype(BF16), b.astype(BF16))


def _split2(x):
    hi = x.astype(BF16)
    lo = (x - hi.astype(F32)).astype(BF16)
    return hi, lo


def _sigmoid(x):
    return 1.0 / (1.0 + jnp.exp(-x))


def _silu(x):
    return x * _sigmoid(x)


def _cparams(n_axes, vmem_limit=VMEM_LIMIT):
    return pltpu.CompilerParams(dimension_semantics=("arbitrary",) * n_axes,
                                vmem_limit_bytes=vmem_limit)


def _ffn_kernel(x_ref, g_ref, wg_ref, wu_ref, wd_ref, fn_ref, o_ref, h_ref, *, final_norm):
    j = pl.program_id(1)
    n_panels = x_ref.shape[0] // FFN_ROW_PANEL
    panel = lambda i: pl.ds(pl.multiple_of(i * FFN_ROW_PANEL, FFN_ROW_PANEL), FFN_ROW_PANEL)

    @pl.when(j == 0)
    def _():
        def norm_panel(i, carry):
            rows = panel(i)
            x = x_ref[rows, :]
            inv = lax.rsqrt(jnp.mean(x * x, axis=-1, keepdims=True) + RMS_EPS)
            h_ref[rows, :] = ((x * inv) * g_ref[...]).astype(BF16)
            return carry

        lax.fori_loop(0, n_panels, norm_panel, 0)
        o_ref[...] = jnp.zeros_like(o_ref)

    h = h_ref[...]
    gate = _dot(h, wg_ref[...])
    up = _dot(h, wu_ref[...])
    act = (_silu(gate) * up).astype(BF16)
    o_ref[...] += _dot(act, wd_ref[...])

    @pl.when(j == pl.num_programs(1) - 1)
    def _():
        def residual_panel(i, carry):
            rows = panel(i)
            y = x_ref[rows, :] + FFN_RES * o_ref[rows, :]
            if final_norm:
                inv = lax.rsqrt(jnp.mean(y * y, axis=-1, keepdims=True) + RMS_EPS)
                y = (y * inv) * fn_ref[...]
            o_ref[rows, :] = y
            return carry

        lax.fori_loop(0, n_panels, residual_panel, 0)


def _ffn(x, gain, wg, wu, wd, fnorm, *, final_norm, tm=1024, tf=512):
    n, d = x.shape
    f = wg.shape[1]
    assert n % tm == 0 and f % tf == 0
    return pl.pallas_call(
        functools.partial(_ffn_kernel, final_norm=final_norm),
        grid=(n // tm, f // tf),
        in_specs=[
            pl.BlockSpec((tm, d), lambda i, j: (i, 0)),
            pl.BlockSpec((1, d), lambda i, j: (0, 0)),
            pl.BlockSpec((d, tf), lambda i, j: (0, j)),
            pl.BlockSpec((d, tf), lambda i, j: (0, j)),
            pl.BlockSpec((tf, d), lambda i, j: (j, 0)),
            pl.BlockSpec((1, d), lambda i, j: (0, 0)),
        ],
        out_specs=pl.BlockSpec((tm, d), lambda i, j: (i, 0)),
        out_shape=jax.ShapeDtypeStruct((n, d), F32),
        scratch_shapes=[pltpu.VMEM((tm, d), BF16)],
        compiler_params=_cparams(2, FFN_VMEM_LIMIT),
        name="ffn_final" if final_norm else "ffn",
    )(x, gain, wg, wu, wd, fnorm)


def _inproj_kernel(x_ref, g_ref, w_ref, o_ref, h_ref):
    @pl.when(pl.program_id(1) == 0)
    def _():
        x = x_ref[...]
        inv = lax.rsqrt(jnp.mean(x * x, axis=-1, keepdims=True) + RMS_EPS)
        h_ref[...] = ((x * inv) * g_ref[...]).astype(BF16)

    o_ref[...] = _dot(h_ref[...], w_ref[...])


def _inproj(x, gain, w, *, tm=1024, tn=1536):
    n, d = x.shape
    c = w.shape[1]
    assert n % tm == 0 and c % tn == 0
    return pl.pallas_call(
        _inproj_kernel,
        grid=(n // tm, c // tn),
        in_specs=[
            pl.BlockSpec((tm, d), lambda i, j: (i, 0)),
            pl.BlockSpec((1, d), lambda i, j: (0, 0)),
            pl.BlockSpec((d, tn), lambda i, j: (0, j)),
        ],
        out_specs=pl.BlockSpec((tm, tn), lambda i, j: (i, j)),
        out_shape=jax.ShapeDtypeStruct((n, c), F32),
        scratch_shapes=[pltpu.VMEM((tm, d), BF16)],
        compiler_params=_cparams(2),
        name="inproj",
    )(x, gain, w)


def _outproj_kernel(x_ref, yr_ref, yh_ref, wr_ref, wh_ref, o_ref):
    o_ref[...] = x_ref[...] + _dot(yr_ref[...], wr_ref[...]) + _dot(yh_ref[...], wh_ref[...])


def _outproj(x, yr, yh, wr, wh, *, tm=512):
    n, d = x.shape
    cr, ch = yr.shape[1], yh.shape[1]
    assert n % tm == 0
    return pl.pallas_call(
        _outproj_kernel,
        grid=(n // tm,),
        in_specs=[
            pl.BlockSpec((tm, d), lambda i: (i, 0)),
            pl.BlockSpec((tm, cr), lambda i: (i, 0)),
            pl.BlockSpec((tm, ch), lambda i: (i, 0)),
            pl.BlockSpec((cr, d), lambda i: (0, 0)),
            pl.BlockSpec((ch, d), lambda i: (0, 0)),
        ],
        out_specs=pl.BlockSpec((tm, d), lambda i: (i, 0)),
        out_shape=jax.ShapeDtypeStruct((n, d), F32),
        compiler_params=_cparams(1),
        name="outproj",
    )(x, yr, yh, wr, wh)


def _rwkv_masks():
    i = np.arange(STACK)[:, None]
    j = np.arange(STACK)[None, :]
    same = (i // CHUNK) == (j // CHUNK)
    sl = same & (i > j)
    il = same & (i >= j)
    blk = lambda s: (i // s) == (j // s)
    d8 = sl & blk(8)
    c16 = sl & blk(16) & ~blk(8)
    c32 = sl & blk(32) & ~blk(16)
    c64 = sl & ~blk(32)
    return np.stack([sl, il, d8, c16, c32, c64]).astype(np.float32)


def _head_sum_matrix():
    i = np.arange(LANES)[:, None]
    j = np.arange(LANES)[None, :]
    return ((i // RW_HEAD) == (j // RW_HEAD)).astype(np.float32)


def _tri64():
    i = np.arange(CHUNK)[:, None]
    j = np.arange(CHUNK)[None, :]
    return (i >= j).astype(np.float32)


def _hgrn_consts():
    i = np.arange(STACK)[:, None]
    j = np.arange(STACK)[None, :]
    same = (i // CHUNK) == (j // CHUNK)
    cum = (same & (i >= j)).astype(np.float32)
    sels = [cum]
    masks = []
    for h in HG_LEVELS:
        m = (i // (2 * h)) * (2 * h) + h - 1
        ref = (same & (j <= m)).astype(np.float32)
        sign = np.where((i % (2 * h)) >= h, 1.0, -1.0).astype(np.float32)
        sels.append(sign * (cum - ref))
        masks.append(((i // (2 * h)) == (j // (2 * h))) & ((i % (2 * h)) >= h) & ((j % (2 * h)) < h))
    masks.append(i == j)
    return (np.concatenate(sels, axis=0), np.stack(masks).astype(np.float32))


def _shift(x, prev8, mu, first):
    rows = lax.broadcasted_iota(jnp.int32, x.shape, 0)
    prev_row = jnp.where(first, 0.0, prev8[7:8, :])
    xp = jnp.where(rows == 0, prev_row, pltpu.roll(x, shift=1, axis=0))
    return x + (xp - x) * mu


def _stack2(x):
    lane = lax.broadcasted_iota(jnp.int32, x.shape, 1)
    lo = lane < RW_HEAD
    return jnp.concatenate([jnp.where(lo, x, 0.0), jnp.where(lo, 0.0, x)], axis=0)


def _unit_lower_inverse(ms, mk_ref):
    shape = ms[0].shape
    eye = (lax.broadcasted_iota(jnp.int32, shape, 0) == lax.broadcasted_iota(jnp.int32, shape, 1)).astype(F32)
    both = lambda f, xs, ys: [f(x, y) for x, y in zip(xs, ys)]
    m8 = [m * mk_ref[2] for m in ms]
    m8_2 = both(_bdot, m8, m8)
    ts = [eye + m for m in m8]
    ts = [t + d for t, d in zip(ts, both(_bdot, ts, m8_2))]
    m8_4 = both(_bdot, m8_2, m8_2)
    ts = [t + d for t, d in zip(ts, both(_bdot, ts, m8_4))]
    for lvl in (3, 4, 5):
        tc = both(_bdot, ts, [m * mk_ref[lvl] for m in ms])
        ts = [t + d for t, d in zip(ts, both(_bdot, tc, ts))]
    return ts


def _rwkv_kernel(r_ref, k_ref, v_ref, lo_ref, rp_ref, kp_ref, vp_ref, lop_ref,
                 mur_ref, muk_ref, muv_ref, mul_ref,
                 w0_ref, a0_ref, kk_ref, ka_ref, rk_ref, lnw_ref, lnb_ref,
                 w2_ref, a2_ref, g2_ref, tri_ref, e2_ref, mk_ref,
                 y_ref,
                 s_ref, *scratch):
    t = pl.program_id(1)
    p = pl.program_id(2)
    first = t == 0
    tb = r_ref.shape[0]
    nch = tb // CHUNK
    ng = r_ref.shape[1] // LANES
    nhalf = len(scratch) // 7
    gh = ng // nhalf
    hw = gh * LANES
    part = lambda ref, h: ref[:, h * hw:(h + 1) * hw]
    lanes = [slice(g * LANES, (g + 1) * LANES) for g in range(gh)]

    @pl.when(first)
    def _():
        for g in range(ng):
            s_ref[p * ng + g] = jnp.zeros((LANES, LANES), F32)

    e2 = e2_ref[...]
    tri = tri_ref[...]
    head_sum = lambda x: jnp.concatenate([_bdot(x[:, ln], e2) for ln in lanes], axis=1)
    lora = _shift(lo_ref[...], lop_ref[...], mul_ref[...], first)
    txw = jnp.tanh(lora[:, 0:LANES]).astype(BF16)
    xa = lora[:, LANES:2 * LANES].astype(BF16)
    sxg = _sigmoid(lora[:, 2 * LANES:4 * LANES]).astype(BF16)

    def prologue(h):
        r_s, k_s, v_s, al_s, be_s, lw_s, _ = scratch[7 * h:7 * h + 7]
        r = _shift(part(r_ref, h), part(rp_ref, h), part(mur_ref, h), first)
        k = _shift(part(k_ref, h), part(kp_ref, h), part(muk_ref, h), first)
        v = _shift(part(v_ref, h), part(vp_ref, h), part(muv_ref, h), first)
        z = part(w0_ref, h) + _dot(txw, part(w2_ref, h))
        w = -(jnp.maximum(-z, 0.0) + jnp.log(1.0 + jnp.exp(-jnp.abs(z)))) - 0.5
        a = _sigmoid(part(a0_ref, h) + _dot(xa, part(a2_ref, h)))
        gate = _dot(sxg, part(g2_ref, h))
        kk = k * part(kk_ref, h)
        nrm = jnp.sqrt(head_sum(kk * kk))
        kk = kk / jnp.maximum(nrm, KK_EPS)
        kmod = k * (1.0 + (a - 1.0) * part(ka_ref, h))
        bonus = head_sum(r * kmod * part(rk_ref, h)) * v
        r_s[...] = r
        k_s[...] = kmod
        v_s[...] = v
        al_s[...] = -kk
        be_s[...] = kk * a
        lw_s[...] = -jnp.exp(w)
        return bonus, gate

    def stages(h):
        r_s, k_s, v_s, al_s, be_s, lw_s, _ = scratch[7 * h:7 * h + 7]
        return _rwkv_stages(r_s, k_s, v_s, al_s, be_s, lw_s, tri, mk_ref, nch, lanes)

    pro = [prologue(h) for h in range(nhalf)]
    st = [stages(h) for h in range(nhalf)]

    ss = [s_ref[p * ng + g] for g in range(ng)]
    for c in range(nch):
        for h in range(nhalf):
            y_s = scratch[7 * h + 6]
            tiles, glast, y_loc, q, a_c, b_c = st[h]
            for gl in range(gh):
                i = c * gh + gl
                rw, ln = tiles[i]
                s = ss[h * gh + gl]
                y = y_loc[i] + _dot_nt(q[i].astype(BF16), s.astype(BF16))
                y_s[rw, ln] = y[0:CHUNK] + y[CHUNK:STACK]
                ss[h * gh + gl] = s * jnp.exp(glast[i]) + _bdot(s, a_c[i]) + b_c[i]
    for g in range(ng):
        s_ref[p * ng + g] = ss[g]

    inv_n = 1.0 / RW_HEAD
    for h in range(nhalf):
        bonus, gate = pro[h]
        y = scratch[7 * h + 6][...]
        mean = head_sum(y) * inv_n
        d = y - mean
        var = head_sum(d * d) * inv_n
        yn = d * lax.rsqrt(var + GN_EPS) * part(lnw_ref, h) + part(lnb_ref, h)
        y_ref[:, h * hw:(h + 1) * hw] = ((yn + bonus) * gate).astype(y_ref.dtype)


def _rwkv_stages(r_s, k_s, v_s, al_s, be_s, lw_s, tri, mk_ref, nch, lanes):
    bf = lambda xs: [x.astype(BF16) for x in xs]
    cat0 = lambda xs, ys: [jnp.concatenate([x, y], axis=0) for x, y in zip(xs, ys)]
    cat1 = lambda xs, ys: [jnp.concatenate([x, y], axis=1) for x, y in zip(xs, ys)]
    tiles = [(pl.ds(c * CHUNK, CHUNK), ln) for c in range(nch) for ln in lanes]
    lwc = [lw_s[rw, ln] for rw, ln in tiles]
    hi, lo = _split2(jnp.concatenate(lwc, axis=1))
    gall = _dot(tri, jnp.concatenate([hi, lo], axis=0))
    gc = [gall[:, i * LANES:(i + 1) * LANES] for i in range(len(tiles))]
    glast = [g_[CHUNK - 1:CHUNK, :] for g_ in gc]
    e_in = [jnp.exp(g_) for g_ in gc]
    e_ex = [jnp.exp(g_ - l_) for g_, l_ in zip(gc, lwc)]
    e_inv = [jnp.exp(-g_) for g_ in gc]
    e_end = [jnp.exp(gl - g_) for gl, g_ in zip(glast, gc)]
    at = [_stack2(al_s[rw, ln] * e) for (rw, ln), e in zip(tiles, e_ex)]
    rt = [_stack2(r_s[rw, ln] * e) for (rw, ln), e in zip(tiles, e_in)]
    bt = [_stack2(be_s[rw, ln] * e) for (rw, ln), e in zip(tiles, e_inv)]
    kt = [_stack2(k_s[rw, ln] * e) for (rw, ln), e in zip(tiles, e_inv)]
    bh = [_stack2(be_s[rw, ln] * e) for (rw, ln), e in zip(tiles, e_end)]
    kh = [_stack2(k_s[rw, ln] * e) for (rw, ln), e in zip(tiles, e_end)]
    vs = [_stack2(v_s[rw, ln]) for rw, ln in tiles]

    g1 = [_dot_nt(x, y) for x, y in zip(bf(cat0(at, rt)), bf(cat0(bt, kt)))]
    m_ab = [g_[0:STACK, 0:STACK] * mk_ref[0] for g_ in g1]
    m_ak = [g_[0:STACK, STACK:2 * STACK] * mk_ref[0] for g_ in g1]
    m_rb = [g_[STACK:2 * STACK, 0:STACK] * mk_ref[1] for g_ in g1]
    m_rk = [g_[STACK:2 * STACK, STACK:2 * STACK] * mk_ref[1] for g_ in g1]

    tinv = _unit_lower_inverse(m_ab, mk_ref)
    x = [_bdot(m, v_) for m, v_ in zip(m_ak, vs)]
    tx = [_bdot(t_, z_) for t_, z_ in zip(tinv, cat1(x, at))]
    u_loc = [t_[:, 0:LANES] for t_ in tx]
    wm = [t_[:, LANES:2 * LANES] for t_ in tx]
    y_loc = [_bdot(m, z_) for m, z_ in zip(cat1(m_rb, m_rk), cat0(u_loc, vs))]
    q = [r_ + _bdot(m, w_) for r_, m, w_ in zip(rt, m_rb, wm)]

    a_c = [_dot_tn(w_, b_) for w_, b_ in zip(bf(wm), bf(bh))]
    b_c = [_dot_tn(z_, w_) for z_, w_ in zip(bf(cat0(u_loc, vs)), bf(cat0(bh, kh)))]
    return tiles, glast, y_loc, q, a_c, b_c


def _rwkv(proj, mu, w0, a0, k_k, k_a, r_k, ln_w, ln_b, w2p, a2p, g2p, *, batch, seq, width, lora_col,
          tb=256, pairs_per_step=8, parts=2):
    n = proj.shape[0]
    nt = seq // tb
    npair = width // LANES
    cw = pairs_per_step * LANES
    wb = width // cw
    lb4 = lora_col // (4 * LANES)
    rb8 = tb // 8
    assert npair % pairs_per_step == 0 and seq % tb == 0

    def main(sec):
        return pl.BlockSpec((tb, cw), lambda b, t, p: (b * nt + t, sec * wb + p))

    def prev(sec):
        return pl.BlockSpec((8, cw), lambda b, t, p: (jnp.maximum((b * nt + t) * rb8 - 1, 0), sec * wb + p))

    def vec(sec):
        return pl.BlockSpec((1, cw), lambda b, t, p: (0, sec * wb + p))

    pvec = pl.BlockSpec((1, cw), lambda b, t, p: (0, p))
    const2 = lambda shape: pl.BlockSpec(shape, lambda b, t, p: (0, 0))
    tri = jnp.asarray(np.concatenate([_tri64(), _tri64()], axis=1), BF16)
    e2 = jnp.asarray(_head_sum_matrix(), BF16)
    masks = jnp.asarray(_rwkv_masks(), F32)
    assert pairs_per_step % parts == 0
    blk = lambda: pltpu.VMEM((tb, cw // parts), F32)
    return pl.pallas_call(
        _rwkv_kernel,
        grid=(batch, nt, wb),
        in_specs=[
            main(0), main(1), main(2),
            pl.BlockSpec((tb, 4 * LANES), lambda b, t, p: (b * nt + t, lb4)),
            prev(0), prev(1), prev(2),
            pl.BlockSpec((8, 4 * LANES), lambda b, t, p: (jnp.maximum((b * nt + t) * rb8 - 1, 0), lb4)),
            vec(0), vec(1), vec(2),
            pl.BlockSpec((1, 4 * LANES), lambda b, t, p: (0, lb4)),
            pvec, pvec, pvec, pvec, pvec, pvec, pvec,
            pl.BlockSpec((LANES, cw), lambda b, t, p: (0, p)),
            pl.BlockSpec((LANES, cw), lambda b, t, p: (0, p)),
            pl.BlockSpec((2 * LANES, cw), lambda b, t, p: (0, p)),
            const2((CHUNK, 2 * CHUNK)), const2((LANES, LANES)),
            pl.BlockSpec((6, STACK, STACK), lambda b, t, p: (0, 0, 0)),
        ],
        out_specs=pl.BlockSpec((tb, cw), lambda b, t, p: (b * nt + t, p)),
        out_shape=jax.ShapeDtypeStruct((n, width), BF16),
        scratch_shapes=[pltpu.VMEM((npair, LANES, LANES), F32)] + [blk() for _ in range(7 * parts)],
        compiler_params=_cparams(3),
        name="rwkv",
    )(proj, proj, proj, proj, proj, proj, proj, proj, mu, mu, mu, mu,
      w0, a0, k_k, k_a, r_k, ln_w, ln_b, w2p, a2p, g2p, tri, e2, masks)


def _hgrn_kernel(q_ref, f_ref, i_ref, g_ref, lg_ref, nw_ref, sel_ref, mk_ref,
                 y_ref,
                 s_ref, q_s, k_s, v_s, lf_s, o_s, hl_s, *, layer):
    t = pl.program_id(1)
    hp = pl.program_id(2)
    tb = q_ref.shape[0]
    nch = tb // CHUNK
    nlev = len(HG_LEVELS)
    w2 = 2 * HG_HEAD
    npr = q_ref.shape[1] // w2

    @pl.when(t == 0)
    def _():
        for pr in range(npr):
            s_ref[hp * npr + pr] = jnp.zeros((2, HG_HEAD, HG_HEAD), F32)

    lg = lg_ref[...]
    mx = jnp.max(lg, axis=0, keepdims=True)
    ex = jnp.exp(lg - mx)
    lb = jnp.sum(ex[0:layer + 1, :], axis=0, keepdims=True) / jnp.sum(ex, axis=0, keepdims=True)
    forget = lb + (1.0 - lb) * _sigmoid(f_ref[...])
    q_s[...] = _silu(q_ref[...])
    k_s[...] = 1.0 - forget
    v_s[...] = i_ref[...]
    lf_s[...] = jnp.log(forget)

    def rows2(ref, tile):
        rw, pr = tile
        blk = ref[rw, pr * w2:(pr + 1) * w2]
        return jnp.concatenate([blk[:, 0:HG_HEAD], blk[:, HG_HEAD:w2]], axis=0)

    tiles = [(pl.ds(c * CHUNK, CHUNK), pr) for c in range(nch) for pr in range(npr)]
    cs = range(len(tiles))
    qc = [rows2(q_s, tl) for tl in tiles]
    kc = [rows2(k_s, tl) for tl in tiles]
    vc = [rows2(v_s, tl) for tl in tiles]
    hi, lo = _split2(jnp.concatenate([rows2(lf_s, tl) for tl in tiles], axis=1))
    hl_s[...] = jnp.concatenate([hi, lo], axis=0)
    sel_all = _dot(sel_ref[...], hl_s[...])
    sel_dot = lambda blk: [sel_all[blk * STACK:(blk + 1) * STACK, c * HG_HEAD:(c + 1) * HG_HEAD] for c in cs]
    gc = sel_dot(0)
    qb = [q_.astype(BF16) for q_ in qc]
    kb = [k_.astype(BF16) for k_ in kc]
    a = [_dot_nt(q_, k_) * mk_ref[nlev] for q_, k_ in zip(qb, kb)]
    for li in range(nlev):
        ed = [jnp.exp(d_).astype(BF16) for d_ in sel_dot(li + 1)]
        a = [a_ + _dot_nt(q_ * e, k_ * e) * mk_ref[li] for a_, q_, k_, e in zip(a, qb, kb, ed)]
    o = [_bdot(a_, v_) for a_, v_ in zip(a, vc)]
    qg = [(q_ * jnp.exp(g_)).astype(BF16) for q_, g_ in zip(qc, gc)]
    halves = []
    new = [[None, None] for _ in range(npr)]
    for h in range(2):
        sl = slice(h * CHUNK, (h + 1) * CHUNK)
        glast = [g_[(h + 1) * CHUNK - 1:(h + 1) * CHUNK, :] for g_ in gc]
        khat = [(k_[sl] * jnp.exp(gl - g_[sl])).astype(BF16) for k_, gl, g_ in zip(kc, glast, gc)]
        delta = [_dot_tn(v_[sl].astype(BF16), k_) for v_, k_ in zip(vc, khat)]
        ss = [s_ref[hp * npr + pr][h] for pr in range(npr)]
        s_in = []
        for c in cs:
            pr = tiles[c][1]
            s_in.append(ss[pr].astype(BF16))
            ss[pr] = ss[pr] * jnp.exp(glast[c]) + delta[c]
        for pr in range(npr):
            new[pr][h] = ss[pr]
        halves.append([o_[sl] + _dot_nt(q_[sl], s_) for o_, q_, s_ in zip(o, qg, s_in)])
    for pr in range(npr):
        s_ref[hp * npr + pr] = jnp.stack(new[pr])
    for c, (rw, pr) in enumerate(tiles):
        o_s[rw, pr * w2:(pr + 1) * w2] = jnp.concatenate([halves[0][c], halves[1][c]], axis=1)

    o = o_s[...]
    gate = _silu(g_ref[...])
    nw = nw_ref[...]
    parts = []
    for h in range(2 * npr):
        oh = o[:, h * HG_HEAD:(h + 1) * HG_HEAD]
        inv = lax.rsqrt(jnp.mean(oh * oh, axis=-1, keepdims=True) + RMS_EPS)
        parts.append(oh * inv)
    y_ref[...] = (jnp.concatenate(parts, axis=1) * nw * gate).astype(y_ref.dtype)


def _hgrn(proj, lb_logits, norm_w, *, layer, batch, seq, width, col0, tb=512, pairs_per_step=4):
    n = proj.shape[0]
    nt = seq // tb
    npair = width // (2 * HG_HEAD)
    cw = pairs_per_step * 2 * HG_HEAD
    cb = col0 // cw
    sec = width // cw
    assert npair % pairs_per_step == 0 and col0 % cw == 0 and seq % tb == 0
    sel_np, mk_np = _hgrn_consts()
    sel = jnp.asarray(np.concatenate([sel_np, sel_np], axis=1), BF16)
    masks = jnp.asarray(mk_np, F32)

    def main(s):
        return pl.BlockSpec((tb, cw), lambda b, t, h: (b * nt + t, cb + s * sec + h))

    blk = lambda: pltpu.VMEM((tb, cw), F32)
    return pl.pallas_call(
        functools.partial(_hgrn_kernel, layer=layer),
        grid=(batch, nt, sec),
        in_specs=[
            main(0), main(1), main(2), main(3),
            pl.BlockSpec((lb_logits.shape[0], cw), lambda b, t, h: (0, h)),
            pl.BlockSpec((1, cw), lambda b, t, h: (0, h)),
            pl.BlockSpec(sel.shape, lambda b, t, h: (0, 0)),
            pl.BlockSpec(masks.shape, lambda b, t, h: (0, 0, 0)),
        ],
        out_specs=pl.BlockSpec((tb, cw), lambda b, t, h: (b * nt + t, h)),
        out_shape=jax.ShapeDtypeStruct((n, width), BF16),
        scratch_shapes=[pltpu.VMEM((npair, 2, HG_HEAD, HG_HEAD), F32)] + [blk() for _ in range(5)]
        + [pltpu.VMEM((2 * STACK, (tb // CHUNK) * pairs_per_step * HG_HEAD), BF16)],
        compiler_params=_cparams(3),
        name="hgrn",
    )(proj, proj, proj, proj, lb_logits, norm_w, sel, masks)


def _pad_cols(w, to):
    return jnp.pad(w, ((0, 0), (0, to - w.shape[1])))


def _pad_rows(w, to):
    return jnp.pad(w, ((0, to - w.shape[0]), (0, 0)))


def kernel(x, ffn1_norm, ffn1_w_gate, ffn1_w_up, ffn1_w_down, mix_norm, w_in,
           rwkv_mu, rwkv_w0, rwkv_w2, rwkv_a0, rwkv_a2, rwkv_g2, rwkv_k_k, rwkv_k_a,
           rwkv_r_k, rwkv_ln_w, rwkv_ln_b, hgrn_lb_logits, hgrn_norm, w_out,
           ffn2_norm, ffn2_w_gate, ffn2_w_up, ffn2_w_down, final_norm):
    batch, seq, d = x.shape
    depth = ffn1_norm.shape[0]
    c = rwkv_w0.shape[1]
    ch = hgrn_norm.shape[1]
    dl, al, gl = rwkv_w2.shape[1], rwkv_a2.shape[1], rwkv_g2.shape[1]
    assert dl <= LANES and al <= LANES and gl <= 2 * LANES

    xf = x.reshape(batch * seq, d)
    row = lambda v: v.reshape(1, -1).astype(F32)
    for l in range(depth):
        xf = _ffn(xf, row(ffn1_norm[l]), ffn1_w_gate[l].astype(BF16), ffn1_w_up[l].astype(BF16),
                  ffn1_w_down[l].astype(BF16), row(final_norm), final_norm=False)

        wi = w_in[l]
        o = 3 * c
        lora_col = o + 4 * ch
        secs = [wi[:, :o], wi[:, o + dl + al + gl:], _pad_cols(wi[:, o:o + dl], LANES),
                _pad_cols(wi[:, o + dl:o + dl + al], LANES),
                _pad_cols(wi[:, o + dl + al:o + dl + al + gl], 2 * LANES)]
        w_cat = jnp.concatenate(secs, axis=1).astype(BF16)
        mu = rwkv_mu[l].reshape(1, -1)
        mu_cat = jnp.concatenate([mu[:, :o], jnp.zeros((1, 4 * ch), F32), _pad_cols(mu[:, o:o + dl], LANES),
                                  _pad_cols(mu[:, o + dl:o + dl + al], LANES),
                                  _pad_cols(mu[:, o + dl + al:o + dl + al + gl], 2 * LANES)], axis=1)
        proj = _inproj(xf, row(mix_norm[l]), w_cat)

        y_r = _rwkv(proj, mu_cat, row(rwkv_w0[l]), row(rwkv_a0[l]), row(rwkv_k_k[l]), row(rwkv_k_a[l]),
                    row(rwkv_r_k[l]), row(rwkv_ln_w[l]), row(rwkv_ln_b[l]),
                    _pad_rows(rwkv_w2[l], LANES).astype(BF16), _pad_rows(rwkv_a2[l], LANES).astype(BF16),
                    _pad_rows(rwkv_g2[l], 2 * LANES).astype(BF16),
                    batch=batch, seq=seq, width=c, lora_col=lora_col)
        y_h = _hgrn(proj, hgrn_lb_logits.astype(F32), row(hgrn_norm[l]), layer=l,
                    batch=batch, seq=seq, width=ch, col0=o)
        wo = w_out[l].astype(BF16)
        xf = _outproj(xf, y_r, y_h, wo[:c], wo[c:])

        xf = _ffn(xf, row(ffn2_norm[l]), ffn2_w_gate[l].astype(BF16), ffn2_w_up[l].astype(BF16),
                  ffn2_w_down[l].astype(BF16), row(final_norm), final_norm=(l == depth - 1))
    return xf.reshape(batch, seq, d)
```
